```python
import jax, jax.numpy as jnp
from jax import lax
import numpy as np

D_MODEL = 1024
BATCH = 32
SEQ = 2048
DEPTH = 4

N_BRANCH = 4
WIDTH = 512
EPS = 1e-6
NEG_INF = -1e30
BIG = 1e30

A_HEADS = 4
A_HEAD_DIM = WIDTH // A_HEADS
A_CHUNK = 64

B_HEADS = 8
B_KV_GROUPS = 2
B_HEAD_DIM = WIDTH // B_HEADS
B_KV_WIDTH = B_KV_GROUPS * B_HEAD_DIM
CMP_LEN = 32
CMP_STRIDE = 16
SEL_LEN = 64
SEL_TOPN = 8
WINDOW = 256
Q_BLOCK = 128

C_BLOCKS = 8
C_BLOCK_DIM = WIDTH // C_BLOCKS
CONV_WIDTH = 4
RG_C = 8.0

D_HEADS = 4
D_QK_DIM = WIDTH // (2 * D_HEADS)
D_V_DIM = WIDTH // D_HEADS
D_CHUNK = 64

SPLIT_SIZES = (
    WIDTH, WIDTH, WIDTH, WIDTH,
    WIDTH, B_KV_WIDTH, B_KV_WIDTH, B_KV_WIDTH, B_KV_WIDTH,
    B_KV_WIDTH, B_KV_WIDTH, 3 * B_HEADS, WIDTH,
    WIDTH, WIDTH,
    D_HEADS * D_QK_DIM, D_HEADS * D_QK_DIM, D_HEADS * D_V_DIM, WIDTH,
    N_BRANCH * D_MODEL,
)
N_IN = sum(SPLIT_SIZES)

kernel_name = 'hybrid_gated_hgrn2_nsa_rglru_retention'


def _rms(x, g):
    xf = x.astype(jnp.float32)
    y = xf * lax.rsqrt(jnp.mean(xf * xf, axis=-1, keepdims=True) + EPS)
    return (y * g.astype(jnp.float32)).astype(x.dtype)


def _group_norm(x, g):
    xc = x - jnp.mean(x, axis=-1, keepdims=True)
    var = jnp.mean(xc * xc, axis=-1, keepdims=True)
    return xc * lax.rsqrt(var + EPS) * g.astype(jnp.float32)


def _to_chunks(a, n_heads, chunk):
    b, s, _ = a.shape
    return a.reshape(b, s // chunk, chunk, n_heads, -1).transpose(0, 3, 1, 2, 4)


def _from_chunks(o):
    b, h, n, c, d = o.shape
    return o.transpose(0, 2, 3, 1, 4).reshape(b, n * c, h, d)


def _chunk_states(decay, kv):
    def step(state, inp):
        d, u = inp
        return d[..., None] * state + u, state
    s0 = jnp.zeros(kv.shape[:2] + kv.shape[3:], kv.dtype)
    _, prev = lax.scan(step, s0, (jnp.moveaxis(decay, 2, 0), jnp.moveaxis(kv, 2, 0)))
    return jnp.moveaxis(prev, 0, 2)


def _lin_combine(c1, c2):
    a1, b1 = c1
    a2, b2 = c2
    return a1 * a2, a2 * b1 + b2


def hgrn2_mixer(q, f_logit, i_in, lb):
    f = lb + (1.0 - lb) * jax.nn.sigmoid(f_logit)
    k = (1.0 - lb) * jax.nn.sigmoid(-f_logit)
    qc, kc, vc, lf = (_to_chunks(a, A_HEADS, A_CHUNK) for a in (q, k, i_in, jnp.log(f)))
    b = jnp.cumsum(lf, axis=3)
    b_ref = b[:, :, :, A_CHUNK // 2 - 1:A_CHUNK // 2]
    b_end = b[:, :, :, -1:]
    causal = jnp.tril(jnp.ones((A_CHUNK, A_CHUNK), bool))
    scores = jnp.einsum('bhntd,bhnsd->bhnts', qc * jnp.exp(b - b_ref), kc * jnp.exp(b_ref - b))
    scores = jnp.where(causal, scores, 0.0)
    o = jnp.einsum('bhnts,bhnsv->bhntv', scores, vc)
    kv = jnp.einsum('bhnsd,bhnsv->bhndv', kc * jnp.exp(b_end - b), vc)
    s_prev = _chunk_states(jnp.exp(b_end[:, :, :, 0]), kv)
    o = o + jnp.einsum('bhntd,bhndv->bhntv', qc * jnp.exp(b), s_prev)
    return _from_chunks(o)


def retention_mixer(q, k, v):
    pos = jnp.arange(D_CHUNK, dtype=jnp.float32)
    log_g = jnp.log1p(-jnp.exp2(-5.0 - jnp.arange(D_HEADS, dtype=jnp.float32)))
    qc = _to_chunks(q, D_HEADS, D_CHUNK) * (D_QK_DIM ** -0.5)
    kc = _to_chunks(k, D_HEADS, D_CHUNK)
    vc = _to_chunks(v, D_HEADS, D_CHUNK)
    rel = pos[:, None] - pos[None, :]
    decay = jnp.where(rel >= 0, jnp.exp(log_g[:, None, None] * jnp.maximum(rel, 0.0)), 0.0)
    scores = jnp.einsum('bhntd,bhnsd->bhnts', qc, kc) * decay[None, :, None]
    o = jnp.einsum('bhnts,bhnsv->bhntv', scores, vc)
    k_dec = kc * jnp.exp(log_g[:, None] * (D_CHUNK - 1.0 - pos))[None, :, None, :, None]
    kv = jnp.einsum('bhnsd,bhnsv->bhndv', k_dec, vc)
    chunk_decay = jnp.broadcast_to(jnp.exp(log_g * D_CHUNK)[None, :, None, None], kv.shape[:4])
    s_prev = _chunk_states(chunk_decay, kv)
    o = o + jnp.einsum('bhntd,bhndv->bhntv', qc, s_prev) * jnp.exp(log_g[:, None] * (pos + 1.0))[None, :, None, :, None]
    return _from_chunks(o)


def rglru_mixer(xr, conv_w, conv_b, w_ra, b_ra, w_ri, b_ri, lam):
    b_, s_, w_ = xr.shape
    xc = lax.conv_general_dilated(
        xr, conv_w.astype(jnp.float32)[:, None, :], (1,), [(CONV_WIDTH - 1, 0)],
        dimension_numbers=('NWC', 'WIO', 'NWC'), feature_group_count=w_) + conv_b
    xb = xc.reshape(b_, s_, C_BLOCKS, C_BLOCK_DIM)
    r = jax.nn.sigmoid(jnp.einsum('bsnc,ncd->bsnd', xb, w_ra).reshape(b_, s_, w_) + b_ra)
    ig = jax.nn.sigmoid(jnp.einsum('bsnc,ncd->bsnd', xb, w_ri).reshape(b_, s_, w_) + b_ri)
    log_a = -RG_C * r * jax.nn.softplus(-lam)
    a = jnp.exp(log_a)
    u = jnp.sqrt(-jnp.expm1(2.0 * log_a)) * (ig * xc)
    _, h = lax.associative_scan(_lin_combine, (a, u), axis=1)
    return h


def nsa_mixer(q, kc, vc, ks, vs, kw, vw, gate_logits, q_g, k_g, cmp_pos, w_ck, w_cv):
    b_, s_ = q.shape[:2]
    H, G, hd = B_HEADS, B_KV_GROUPS, B_HEAD_DIM
    R = H // G
    scale = hd ** -0.5
    slopes = jnp.exp2(-8.0 * (jnp.arange(H, dtype=jnp.float32) + 1.0) / H).reshape(G, R)
    heads = lambda a: a.reshape(b_, s_, G, hd)
    qh = _rms(q.reshape(b_, s_, H, hd), q_g)

    n_c = (s_ - CMP_LEN) // CMP_STRIDE + 1
    cmp_idx = jnp.arange(n_c)[:, None] * CMP_STRIDE + jnp.arange(CMP_LEN)[None]
    kblk = heads(kc)[:, cmp_idx] + cmp_pos[None, None, :, None, :]
    vblk = heads(vc)[:, cmp_idx] + cmp_pos[None, None, :, None, :]
    k_cmp = _rms(jnp.einsum('bnlgd,lde->bnge', kblk, w_ck.reshape(CMP_LEN, hd, hd)), k_g)
    v_cmp = jnp.einsum('bnlgd,lde->bnge', vblk, w_cv.reshape(CMP_LEN, hd, hd))
    cmp_end = cmp_idx[:, -1]

    n_sel = s_ // SEL_LEN
    top_n = min(SEL_TOPN, n_sel)
    ks_blk = _rms(heads(ks), k_g).reshape(b_, n_sel, SEL_LEN, G, hd).transpose(0, 3, 1, 2, 4)
    vs_blk = heads(vs).reshape(b_, n_sel, SEL_LEN, G, hd).transpose(0, 3, 1, 2, 4)
    cmp_start = jnp.arange(n_c) * CMP_STRIDE
    sel_start = jnp.arange(n_sel) * SEL_LEN
    overlap = ((cmp_start[:, None] < sel_start[None] + SEL_LEN)
               & (cmp_start[:, None] + CMP_LEN > sel_start[None])).astype(jnp.float32)

    pad = ((0, 0), (WINDOW, 0), (0, 0), (0, 0))
    kw_pad = jnp.pad(_rms(heads(kw), k_g), pad)
    vw_pad = jnp.pad(heads(vw), pad)

    n_qb = s_ // Q_BLOCK
    q_blocks = qh.reshape(b_, n_qb, Q_BLOCK, G, R, hd).transpose(1, 0, 2, 3, 4, 5)
    starts = jnp.arange(n_qb) * Q_BLOCK
    bi = jnp.arange(b_)[:, None, None, None]
    gi = jnp.arange(G)[None, :, None, None]
    jsel = jnp.arange(n_sel)

    def block(args):
        qb, start = args
        t = start + jnp.arange(Q_BLOCK)
        dist_c = t[:, None] - cmp_end[None]
        m_c = dist_c >= 0
        s_c = jnp.einsum('btgrd,bngd->bgrtn', qb, k_cmp) * scale - slopes[None, :, :, None, None] * dist_c
        p_c = jax.nn.softmax(jnp.where(m_c, s_c, NEG_INF), axis=-1) * jnp.any(m_c, axis=-1)[:, None]
        o_c = jnp.einsum('bgrtn,bngd->btgrd', p_c, v_cmp)
        imp = jnp.einsum('bgrtn,nj->bgtj', p_c, overlap)
        blk_t = t // SEL_LEN
        imp = jnp.where(jsel[None] == blk_t[:, None], BIG, jnp.where(jsel[None] < blk_t[:, None], imp, -BIG))
        _, idx = lax.top_k(imp, top_n)
        k_sel = ks_blk[bi, gi, idx]
        v_sel = vs_blk[bi, gi, idx]
        dist_s = t[None, None, :, None, None] - (idx[..., None] * SEL_LEN + jnp.arange(SEL_LEN))
        s_s = jnp.einsum('btgrd,bgtnld->bgrtnl', qb, k_sel) * scale - slopes[None, :, :, None, None, None] * dist_s[:, :, None]
        s_s = jnp.where(dist_s[:, :, None] >= 0, s_s, NEG_INF)
        p_s = jax.nn.softmax(s_s.reshape(s_s.shape[:4] + (top_n * SEL_LEN,)), axis=-1).reshape(s_s.shape)
        o_s = jnp.einsum('bgrtnl,bgtnld->btgrd', p_s, v_sel)
        k_w = lax.dynamic_slice_in_dim(kw_pad, start, Q_BLOCK + WINDOW, axis=1)
        v_w = lax.dynamic_slice_in_dim(vw_pad, start, Q_BLOCK + WINDOW, axis=1)
        spos = start - WINDOW + jnp.arange(Q_BLOCK + WINDOW)
        dist_w = t[:, None] - spos[None]
        m_w = (dist_w >= 0) & (dist_w < WINDOW) & (spos[None] >= 0)
        s_w = jnp.einsum('btgrd,bsgd->bgrts', qb, k_w) * scale - slopes[None, :, :, None, None] * dist_w
        p_w = jax.nn.softmax(jnp.where(m_w, s_w, NEG_INF), axis=-1)
        o_w = jnp.einsum('bgrts,bsgd->btgrd', p_w, v_w)
        return jnp.stack([o_c, o_s, o_w], axis=-2)

    o = lax.map(block, (q_blocks, starts))
    o = o.transpose(1, 0, 2, 3, 4, 5, 6).reshape(b_, s_, H, 3, hd)
    gates = jax.nn.sigmoid(gate_logits.reshape(b_, s_, H, 3))
    return jnp.einsum('bshc,bshcd->bshd', gates, o).reshape(b_, s_, WIDTH)


def setup_inputs(seed: int = 0) -> dict:
    key = jax.random.key(seed)
    k = jax.random.split(key, 21)
    f32 = jnp.float32
    nrm = lambda kk, shape, s: jax.random.normal(kk, shape, f32) * s
    hd = B_HEAD_DIM
    a_c = jax.random.uniform(k[14], (DEPTH, WIDTH), f32, 0.9, 0.999)
    sig_lam = a_c ** (1.0 / RG_C)
    return {
        'x': nrm(k[0], (BATCH, SEQ, D_MODEL), 1.0),
        'norm_g': 1.0 + nrm(k[1], (DEPTH, D_MODEL), 0.02),
        'w_in': nrm(k[2], (DEPTH, D_MODEL, N_IN), D_MODEL ** -0.5),
        'lb_logits': nrm(k[3], (DEPTH, WIDTH), 0.1),
        'a_norm_g': 1.0 + nrm(k[4], (DEPTH, WIDTH), 0.02),
        'b_q_norm_g': 1.0 + nrm(k[5], (DEPTH, hd), 0.02),
        'b_k_norm_g': 1.0 + nrm(k[6], (DEPTH, hd), 0.02),
        'b_cmp_pos': nrm(k[7], (DEPTH, CMP_LEN, hd), 0.02),
        'b_cmp_wk': nrm(k[8], (DEPTH, CMP_LEN * hd, hd), (CMP_LEN * hd) ** -0.5),
        'b_cmp_wv': nrm(k[9], (DEPTH, CMP_LEN * hd, hd), (CMP_LEN * hd) ** -0.5),
        'c_conv_w': nrm(k[10], (DEPTH, CONV_WIDTH, WIDTH), CONV_WIDTH ** -0.5),
        'c_conv_b': nrm(k[11], (DEPTH, WIDTH), 0.01),
        'c_w_ra': nrm(k[12], (DEPTH, C_BLOCKS, C_BLOCK_DIM, C_BLOCK_DIM), C_BLOCK_DIM ** -0.5),
        'c_b_ra': nrm(k[13], (DEPTH, WIDTH), 0.01),
        'c_w_ri': nrm(k[15], (DEPTH, C_BLOCKS, C_BLOCK_DIM, C_BLOCK_DIM), C_BLOCK_DIM ** -0.5),
        'c_b_ri': nrm(k[16], (DEPTH, WIDTH), 0.01),
        'c_lambda': jnp.log(sig_lam) - jnp.log1p(-sig_lam),
        'd_norm_g': 1.0 + nrm(k[17], (DEPTH, WIDTH), 0.02),
        'merge_b': nrm(k[18], (DEPTH, N_BRANCH, D_MODEL), 0.01),
        'w_branch': nrm(k[19], (DEPTH, N_BRANCH, WIDTH, D_MODEL), WIDTH ** -0.5),
        'w_out': nrm(k[20], (DEPTH, D_MODEL, D_MODEL), D_MODEL ** -0.5),
    }


def reference(x, norm_g, w_in, lb_logits, a_norm_g, b_q_norm_g, b_k_norm_g, b_cmp_pos, b_cmp_wk, b_cmp_wv,
              c_conv_w, c_conv_b, c_w_ra, c_b_ra, c_w_ri, c_b_ri, c_lambda, d_norm_g, merge_b, w_branch, w_out):
    b_, s_, _ = x.shape
    f32 = jnp.float32
    p_lb = jax.nn.softmax(lb_logits.astype(f32), axis=0)
    lower_bounds = jnp.cumsum(p_lb, axis=0) - p_lb[0:1]
    split_points = np.cumsum(SPLIT_SIZES)[:-1]
    for l in range(DEPTH):
        xn = _rms(x, norm_g[l])
        u = (xn @ w_in[l]).astype(f32)
        (a_q, a_f, a_i, a_z, b_q, b_kc, b_vc, b_ks, b_vs, b_kw, b_vw, b_g, b_z,
         c_x, c_z, d_q, d_k, d_v, d_z, m_g) = jnp.split(u, split_points, axis=-1)

        o_a = _rms(hgrn2_mixer(a_q, a_f, a_i, lower_bounds[l]), a_norm_g[l].reshape(A_HEADS, A_HEAD_DIM))
        o_a = o_a.reshape(b_, s_, WIDTH) * jax.nn.silu(a_z)
        o_b = nsa_mixer(b_q, b_kc, b_vc, b_ks, b_vs, b_kw, b_vw, b_g, b_q_norm_g[l], b_k_norm_g[l],
                        b_cmp_pos[l], b_cmp_wk[l], b_cmp_wv[l]) * jax.nn.silu(b_z)
        o_c = rglru_mixer(c_x, c_conv_w[l], c_conv_b[l], c_w_ra[l], c_b_ra[l], c_w_ri[l], c_b_ri[l],
                          c_lambda[l]) * jax.nn.silu(c_z)
        o_d = _group_norm(retention_mixer(d_q, d_k, d_v), d_norm_g[l].reshape(D_HEADS, D_V_DIM))
        o_d = o_d.reshape(b_, s_, WIDTH) * jax.nn.silu(d_z)

        merged = 0.0
        for br, o_k in enumerate((o_a, o_b, o_c, o_d)):
            gate = jax.nn.sigmoid(m_g[..., br * D_MODEL:(br + 1) * D_MODEL] + merge_b[l, br])
            merged = merged + gate * (o_k.astype(x.dtype) @ w_branch[l, br])
        x = x + merged.astype(x.dtype) @ w_out[l]
    return x
```

```python
import functools

import numpy as np
import jax
import jax.numpy as jnp
from jax import lax
from jax.experimental import pallas as pl
from jax.experimental.pallas import tpu as pltpu

F32 = jnp.float32
BF16 = jnp.bfloat16

D_MODEL = 1024
WIDTH = 512
EPS = 1e-6
NEG_INF = -1e30
BIG = 1e30

A_HEADS = 4
A_HEAD_DIM = 128
CHUNK = 64

B_HEADS = 8
B_GROUPS = 2
B_REP = B_HEADS // B_GROUPS
B_HEAD_DIM = 64
B_KV = B_GROUPS * B_HEAD_DIM
CMP_LEN = 32
CMP_STRIDE = 16
SEL_LEN = 64
SEL_TOPN = 8
WINDOW = 256
Q_BLOCK = 128

C_BLOCKS = 8
C_BLOCK_DIM = 64
CONV_WIDTH = 4
RG_C = 8.0

D_HEADS = 4
D_QK = 64
D_V = 128

N_BRANCH = 4

TS = 512
LANES = 128
VMEM_LIMIT = 56 * 1024 * 1024

B_HEAD_PERM = (0, 4, 1, 5, 2, 6, 3, 7)
B_SLOPES = tuple(float(2.0 ** (-8.0 * (h + 1) / B_HEADS)) for h in range(B_HEADS))


def _dot(a, b):
    return jnp.dot(a.astype(BF16), b.astype(BF16), preferred_element_type=F32)


def _dot_nt(a, b):
    return lax.dot_general(a.astype(BF16), b.astype(BF16), (((1,), (1,)), ((), ())),
                           preferred_element_type=F32)


def _dot_tn(a, b):
    return lax.dot_general(a.astype(BF16), b.astype(BF16), (((0,), (0,)), ((), ())),
                           preferred_element_type=F32)


def _rms_rows(x, g):
    ms = jnp.mean(x * x, axis=-1, keepdims=True)
    return x * lax.rsqrt(ms + EPS) * g


def _silu(z):
    return z * jax.nn.sigmoid(z)


def _iota(shape, dim):
    return lax.broadcasted_iota(jnp.int32, shape, dim)


def _normed_input(x_ref, g_ref):
    return _rms_rows(x_ref[...], g_ref[...]).astype(BF16)


def _hgrn_kernel(x_ref, g_ref, w_ref, lb_ref, ng_ref, o_ref, q_s, k_s, v_s, b_s, z_s, st_s, o_s):
    @pl.when(pl.program_id(1) == 0)
    def _():
        st_s[...] = jnp.zeros_like(st_s)

    u = jnp.dot(_normed_input(x_ref, g_ref), w_ref[...], preferred_element_type=F32)
    fl = u[:, WIDTH:2 * WIDTH]
    lb = lb_ref[...]
    f = lb + (1.0 - lb) * jax.nn.sigmoid(fl)
    q_s[...] = u[:, 0:WIDTH]
    k_s[...] = (1.0 - lb) * jax.nn.sigmoid(-fl)
    v_s[...] = u[:, 2 * WIDTH:3 * WIDTH]
    z_s[...] = u[:, 3 * WIDTH:4 * WIDTH]
    b = jnp.log(f)
    row_in_chunk = _iota((TS, WIDTH), 0) & (CHUNK - 1)
    s = 1
    while s < CHUNK:
        b = b + jnp.where(row_in_chunk >= s, pltpu.roll(b, s, 0), 0.0)
        s *= 2
    b_s[...] = b

    causal = _iota((CHUNK, CHUNK), 0) >= _iota((CHUNK, CHUNK), 1)

    def chunk(c, carry):
        r0 = pl.multiple_of(c * CHUNK, CHUNK)
        rows = pl.ds(r0, CHUNK)
        qc, kc, vc, bc = q_s[rows, :], k_s[rows, :], v_s[rows, :], b_s[rows, :]
        b_mid = b_s[pl.ds(r0 + CHUNK // 2 - 1, 1), :]
        b_end = b_s[pl.ds(r0 + CHUNK - 1, 1), :]
        qe = (qc * jnp.exp(bc - b_mid)).astype(BF16)
        ke = (kc * jnp.exp(b_mid - bc)).astype(BF16)
        k_end = (kc * jnp.exp(b_end - bc)).astype(BF16)
        qb = (qc * jnp.exp(bc)).astype(BF16)
        dec = jnp.exp(b_end)
        vb = vc.astype(BF16)
        for h in range(A_HEADS):
            sl = slice(h * A_HEAD_DIM, (h + 1) * A_HEAD_DIM)
            sc = jnp.where(causal, _dot_nt(qe[:, sl], ke[:, sl]), 0.0)
            st = st_s[h]
            o_s[rows, sl] = _dot(sc, vb[:, sl]) + _dot_nt(qb[:, sl], st)
            st_s[h] = st * dec[:, sl] + _dot_tn(vb[:, sl], k_end[:, sl])
        return carry

    lax.fori_loop(0, TS // CHUNK, chunk, 0)

    ng = ng_ref[...]
    for h in range(A_HEADS):
        sl = slice(h * A_HEAD_DIM, (h + 1) * A_HEAD_DIM)
        o_ref[:, sl] = (_rms_rows(o_s[:, sl], ng[:, sl]) * _silu(z_s[:, sl])).astype(o_ref.dtype)


def _ret_kernel(x_ref, g_ref, w_ref, dmat_ref, kdec_ref, qdec_ref, cdec_ref, ng_ref, o_ref,
                q_s, k_s, v_s, z_s, st_s, o_s):
    @pl.when(pl.program_id(1) == 0)
    def _():
        st_s[...] = jnp.zeros_like(st_s)

    u = jnp.dot(_normed_input(x_ref, g_ref), w_ref[...], preferred_element_type=F32)
    nqk = D_HEADS * D_QK
    q_s[...] = u[:, 0:nqk] * (D_QK ** -0.5)
    k_s[...] = u[:, nqk:2 * nqk]
    v_s[...] = u[:, 2 * nqk:2 * nqk + WIDTH]
    z_s[...] = u[:, 2 * nqk + WIDTH:2 * nqk + 2 * WIDTH]
    first_half = _iota((CHUNK, LANES), 1) < D_QK

    def chunk(c, carry):
        rows = pl.ds(pl.multiple_of(c * CHUNK, CHUNK), CHUNK)
        qc, kc, vc = q_s[rows, :], k_s[rows, :], v_s[rows, :]
        kd = (kc * kdec_ref[...]).astype(BF16)
        kb = kc.astype(BF16)
        vb = vc.astype(BF16)
        qdec = qdec_ref[...]
        cdec = cdec_ref[...]
        for h in range(D_HEADS):
            pair = slice((h // 2) * LANES, (h // 2 + 1) * LANES)
            vsl = slice(h * D_V, (h + 1) * D_V)
            qm = jnp.where(first_half if h % 2 == 0 else jnp.logical_not(first_half), qc[:, pair], 0.0)
            qm = qm.astype(BF16)
            sc = _dot_nt(qm, kb[:, pair]) * dmat_ref[h]
            st = st_s[h]
            o_s[rows, vsl] = _dot(sc, vb[:, vsl]) + _dot(qm, st) * qdec[:, vsl]
            st_s[h] = st * cdec[:, vsl] + _dot_tn(kd[:, pair], vb[:, vsl])
        return carry

    lax.fori_loop(0, TS // CHUNK, chunk, 0)

    ng = ng_ref[...]
    for h in range(D_HEADS):
        sl = slice(h * D_V, (h + 1) * D_V)
        o = o_s[:, sl]
        oc = o - jnp.mean(o, axis=-1, keepdims=True)
        var = jnp.mean(oc * oc, axis=-1, keepdims=True)
        o_ref[:, sl] = (oc * lax.rsqrt(var + EPS) * ng[:, sl] * _silu(z_s[:, sl])).astype(o_ref.dtype)


CONV_PAD = 8


def _rglru_kernel(x_ref, g_ref, w_ref, cw_ref, cb_ref, wra_ref, bra_ref, wri_ref, bri_ref, lam_ref,
                  o_ref, ext_s, h_s):
    t = pl.program_id(1)

    @pl.when(t == 0)
    def _():
        ext_s[0:CONV_PAD, :] = jnp.zeros((CONV_PAD, WIDTH), F32)
        h_s[...] = jnp.zeros_like(h_s)

    @pl.when(t > 0)
    def _():
        ext_s[0:CONV_PAD, :] = ext_s[TS:TS + CONV_PAD, :]

    u = jnp.dot(_normed_input(x_ref, g_ref), w_ref[...], preferred_element_type=F32)
    ext_s[CONV_PAD:CONV_PAD + TS, :] = u[:, 0:WIDTH]
    cz = u[:, WIDTH:2 * WIDTH]
    cw = cw_ref[...]
    xc = cb_ref[...] + cw[CONV_WIDTH - 1:CONV_WIDTH, :] * u[:, 0:WIDTH]
    for j in range(CONV_WIDTH - 1):
        back = CONV_WIDTH - 1 - j
        xc = xc + cw[j:j + 1, :] * ext_s[CONV_PAD - back:CONV_PAD - back + TS, :]
    r = jax.nn.sigmoid(_dot(xc, wra_ref[...]) + bra_ref[...])
    ig = jax.nn.sigmoid(_dot(xc, wri_ref[...]) + bri_ref[...])
    neg_lam = -lam_ref[...]
    softplus = jnp.maximum(neg_lam, 0.0) + jnp.log1p(jnp.exp(-jnp.abs(neg_lam)))
    log_a = -RG_C * r * softplus
    a = jnp.exp(log_a)
    th = jnp.tanh(log_a)
    b = jnp.sqrt(-2.0 * th / (1.0 - th)) * (ig * xc)
    row = _iota((TS, WIDTH), 0)
    s = 1
    while s < TS:
        keep = row >= s
        a_prev = jnp.where(keep, pltpu.roll(a, s, 0), 1.0)
        b_prev = jnp.where(keep, pltpu.roll(b, s, 0), 0.0)
        b = a * b_prev + b
        a = a * a_prev
        s *= 2
    h = a * h_s[0:1, :] + b
    h_s[0:1, :] = h[TS - 1:TS, :]
    o_ref[...] = (h * _silu(cz)).astype(o_ref.dtype)


NSA_Q = 0
NSA_KC = 512
NSA_VC = 640
NSA_KS = 768
NSA_VS = 896
NSA_KW = 1024
NSA_VW = 1152
NSA_G = 1280
NSA_Z = 1408
NSA_COLS = 1920
SEL_KEYS = 512
WIN_KEYS = Q_BLOCK + WINDOW
CMP_PHASES = CMP_STRIDE


def _half_rms(v, first_half):
    v2 = v * v
    s0 = jnp.sum(jnp.where(first_half, v2, 0.0), axis=-1, keepdims=True)
    s1 = jnp.sum(jnp.where(first_half, 0.0, v2), axis=-1, keepdims=True)
    ms = jnp.where(first_half, s0, s1) * (1.0 / B_HEAD_DIM)
    return v * lax.rsqrt(ms + EPS)


def _softmax_rows(s, valid):
    s = jnp.where(valid, s, NEG_INF)
    m = jnp.max(s, axis=-1, keepdims=True)
    p = jnp.exp(s - m)
    return p / jnp.sum(p, axis=-1, keepdims=True)


def _nsa_kernel(seq, x_ref, g_ref, w_ref, qg_ref, kg_ref, pos1_ref, pos2_ref, w1k_ref, w2k_ref,
                w1v_ref, w2v_ref, ovt_ref, gexp_ref, o_ref,
                u_s, ks_c, vs_c, kw_c, vw_c, kc_c, vc_c, p1k_s, p1v_s):
    n_cmp = seq // CMP_STRIDE
    n_sel = seq // SEL_LEN
    ti = pl.program_id(1)

    @pl.when(ti == 0)
    def _():
        for ref in (ks_c, vs_c, kw_c, vw_c, kc_c, vc_c, p1k_s, p1v_s):
            ref[...] = jnp.zeros_like(ref)

    u_s[...] = jnp.dot(_normed_input(x_ref, g_ref), w_ref[...], preferred_element_type=F32)
    row0 = pl.multiple_of(ti * TS, TS)
    kg = kg_ref[...]
    half_t = _iota((TS, LANES), 1) < B_HEAD_DIM

    ks_c[pl.ds(row0, TS), :] = (_half_rms(u_s[:, NSA_KS:NSA_KS + B_KV], half_t) * kg).astype(BF16)
    vs_c[pl.ds(row0, TS), :] = u_s[:, NSA_VS:NSA_VS + B_KV].astype(BF16)
    kw_c[pl.ds(row0 + WINDOW, TS), :] = (_half_rms(u_s[:, NSA_KW:NSA_KW + B_KV], half_t) * kg).astype(BF16)
    vw_c[pl.ds(row0 + WINDOW, TS), :] = u_s[:, NSA_VW:NSA_VW + B_KV].astype(BF16)

    n_grp = TS // CMP_STRIDE
    wide = CMP_PHASES * B_KV
    phase_match = (_iota((TS, wide), 0) & (CMP_STRIDE - 1)) == (_iota((TS, wide), 1) >> 7)
    pool = jnp.where((_iota((n_grp, TS), 1) >> 4) == _iota((n_grp, TS), 0), 1.0, 0.0).astype(BF16)
    first_row = _iota((n_grp, B_KV), 0) == 0
    half_g = _iota((n_grp, LANES), 1) < B_HEAD_DIM
    crow = pl.ds(pl.multiple_of(ti * n_grp, n_grp), n_grp)
    for col, w1_ref, w2_ref, p1_s, cache, is_key in ((NSA_KC, w1k_ref, w2k_ref, p1k_s, kc_c, True),
                                                      (NSA_VC, w1v_ref, w2v_ref, p1v_s, vc_c, False)):
        src = u_s[:, col:col + B_KV]
        spread = jnp.where(phase_match, jnp.concatenate([src] * CMP_PHASES, axis=1), 0.0).astype(BF16)
        grouped = jnp.dot(pool, spread, preferred_element_type=F32)
        c1 = _dot(pos1_ref[...], w1_ref[...])[0:1, :]
        c2 = _dot(pos2_ref[...], w2_ref[...])[0:1, :]
        p1 = _dot(grouped, w1_ref[...]) + c1
        p2 = _dot(grouped, w2_ref[...]) + c2
        p1_prev = jnp.where(first_row, p1_s[n_grp - 1:n_grp, :], pltpu.roll(p1, 1, 0))
        p1_s[...] = p1
        blk = p1_prev + p2
        if is_key:
            blk = _half_rms(blk, half_g) * kg
        cache[crow, :] = blk.astype(BF16)

    half_q = _iota((Q_BLOCK, LANES), 1) < B_HEAD_DIM
    qg = qg_ref[...]

    def sub_block(sb, carry):
        r0 = pl.multiple_of(sb * Q_BLOCK, Q_BLOCK)
        start = row0 + r0
        tiles = []
        for r in range(B_REP):
            qt = u_s[pl.ds(r0, Q_BLOCK), NSA_Q + r * LANES:NSA_Q + (r + 1) * LANES]
            tiles.append(_half_rms(qt, half_q) * qg[:, r * LANES:(r + 1) * LANES] * (B_HEAD_DIM ** -0.5))
        q_grp = [jnp.concatenate([jnp.where(half_q, t, 0.0) for t in tiles], axis=0).astype(BF16),
                 jnp.concatenate([jnp.where(half_q, 0.0, t) for t in tiles], axis=0).astype(BF16)]

        tq_c = start + _iota((Q_BLOCK, n_cmp), 0)
        nprime = _iota((Q_BLOCK, n_cmp), 1)
        dist_c = tq_c - (nprime * CMP_STRIDE + (CMP_STRIDE - 1))
        valid_c = jnp.logical_and(dist_c >= 0, nprime >= 1)
        dist_cf = dist_c.astype(F32)
        any_c = jnp.where(tq_c >= CMP_LEN - 1, 1.0, 0.0)
        kcmp, vcmp = kc_c[...], vc_c[...]
        o_cmp, sel_masks = [], []
        tq_t = start + _iota((n_sel, Q_BLOCK), 1)
        blk_t = tq_t >> 6
        jrow = _iota((n_sel, Q_BLOCK), 0)
        for g in range(B_GROUPS):
            sc = _dot_nt(q_grp[g], kcmp)
            p_heads = []
            for r in range(B_REP):
                slope = B_SLOPES[g * B_REP + r]
                p = _softmax_rows(sc[r * Q_BLOCK:(r + 1) * Q_BLOCK] - slope * dist_cf, valid_c) * any_c
                p_heads.append(p)
            o_cmp.append(_dot(jnp.concatenate(p_heads, axis=0), vcmp))
            p_sum = p_heads[0] + p_heads[1] + p_heads[2] + p_heads[3]
            imp = lax.dot_general(ovt_ref[...], p_sum, (((1,), (1,)), ((), ())),
                                  precision=lax.Precision.HIGHEST, preferred_element_type=F32)
            imp = jnp.where(jrow == blk_t, BIG, jnp.where(jrow < blk_t, imp, -BIG))
            cnt = jnp.zeros((n_sel, Q_BLOCK), F32)
            for jp in range(n_sel):
                other = imp[jp:jp + 1, :]
                ge = jnp.where(other >= imp, 1.0, 0.0)
                gt = jnp.where(other > imp, 1.0, 0.0)
                cnt = cnt + jnp.where(jrow > jp, ge, gt)
            sel_t = jnp.where(cnt < SEL_TOPN, 1.0, 0.0)
            sel_pad = jnp.concatenate([sel_t, jnp.zeros((LANES - n_sel, Q_BLOCK), F32)], axis=0)
            sel_masks.append(sel_pad.T.astype(BF16))

        def sel_step(kc, st):
            k0 = pl.multiple_of(kc * SEL_KEYS, SEL_KEYS)
            keys = ks_c[pl.ds(k0, SEL_KEYS), :]
            vals = vs_c[pl.ds(k0, SEL_KEYS), :]
            keypos = k0 + _iota((Q_BLOCK, SEL_KEYS), 1)
            dist = (start + _iota((Q_BLOCK, SEL_KEYS), 0)) - keypos
            dist_f = dist.astype(F32)
            causal = dist >= 0
            expand = jnp.where((keypos >> 6) == _iota((LANES, SEL_KEYS), 0), 1.0, 0.0).astype(BF16)
            out = []
            for g in range(B_GROUPS):
                m_old, l_old, acc_old = st[3 * g], st[3 * g + 1], st[3 * g + 2]
                picked = jnp.dot(sel_masks[g], expand, preferred_element_type=F32)
                valid = jnp.where(causal, picked, 0.0) > 0.5
                sc = _dot_nt(q_grp[g], keys)
                m_new, l_new, p_all = [], [], []
                for r in range(B_REP):
                    rs = slice(r * Q_BLOCK, (r + 1) * Q_BLOCK)
                    s = jnp.where(valid, sc[rs] - B_SLOPES[g * B_REP + r] * dist_f, NEG_INF)
                    m_r = jnp.maximum(m_old[rs], jnp.max(s, axis=-1, keepdims=True))
                    p = jnp.where(valid, jnp.exp(s - m_r), 0.0)
                    alpha = jnp.exp(m_old[rs] - m_r)
                    m_new.append(m_r)
                    l_new.append(alpha * l_old[rs] + jnp.sum(p, axis=-1, keepdims=True))
                    p_all.append(p)
                m_cat = jnp.concatenate(m_new, axis=0)
                alpha_cat = jnp.exp(m_old - m_cat)
                acc = alpha_cat * acc_old + _dot(jnp.concatenate(p_all, axis=0), vals)
                out += [m_cat, jnp.concatenate(l_new, axis=0), acc]
            return tuple(out)

        init = []
        for g in range(B_GROUPS):
            init += [jnp.full((B_REP * Q_BLOCK, 1), NEG_INF, F32), jnp.zeros((B_REP * Q_BLOCK, 1), F32),
                     jnp.zeros((B_REP * Q_BLOCK, LANES), F32)]
        st = lax.fori_loop(0, ti + 1, sel_step, tuple(init))
        o_sel = [st[3 * g + 2] / st[3 * g + 1] for g in range(B_GROUPS)]

        kwin = kw_c[pl.ds(start, WIN_KEYS), :]
        vwin = vw_c[pl.ds(start, WIN_KEYS), :]
        spos = (start - WINDOW) + _iota((Q_BLOCK, WIN_KEYS), 1)
        dist_w = (start + _iota((Q_BLOCK, WIN_KEYS), 0)) - spos
        valid_w = jnp.where(dist_w >= 0, jnp.where(dist_w < WINDOW, jnp.where(spos >= 0, 1.0, 0.0), 0.0), 0.0) > 0.5
        dist_wf = dist_w.astype(F32)
        o_win = []
        for g in range(B_GROUPS):
            sc = _dot_nt(q_grp[g], kwin)
            p_heads = [_softmax_rows(sc[r * Q_BLOCK:(r + 1) * Q_BLOCK] - B_SLOPES[g * B_REP + r] * dist_wf, valid_w)
                       for r in range(B_REP)]
            o_win.append(_dot(jnp.concatenate(p_heads, axis=0), vwin))

        gates = jax.nn.sigmoid(u_s[pl.ds(r0, Q_BLOCK), NSA_G:NSA_G + LANES])
        gfull = jnp.dot(gates, gexp_ref[...], precision=lax.Precision.HIGHEST,
                        preferred_element_type=F32)
        for r in range(B_REP):
            rs = slice(r * Q_BLOCK, (r + 1) * Q_BLOCK)
            acc = jnp.zeros((Q_BLOCK, LANES), F32)
            for c, branch in enumerate((o_cmp, o_sel, o_win)):
                both = jnp.where(half_q, branch[0][rs], branch[1][rs])
                acc = acc + gfull[:, c * WIDTH + r * LANES:c * WIDTH + (r + 1) * LANES] * both
            z = u_s[pl.ds(r0, Q_BLOCK), NSA_Z + r * LANES:NSA_Z + (r + 1) * LANES]
            o_ref[pl.ds(r0, Q_BLOCK), r * LANES:(r + 1) * LANES] = (acc * _silu(z)).astype(o_ref.dtype)
        return carry

    lax.fori_loop(0, TS // Q_BLOCK, sub_block, 0)


def _merge_kernel(x_ref, oa_ref, ob_ref, oc_ref, od_ref, g_ref, wmg_ref, mb_ref, wbr_ref, wout_ref, o_ref):
    x = x_ref[...]
    xn = _rms_rows(x, g_ref[...]).astype(BF16)
    merged = jnp.zeros((TS, D_MODEL), F32)
    for br, o_k in enumerate((oa_ref, ob_ref, oc_ref, od_ref)):
        cols = slice(br * D_MODEL, (br + 1) * D_MODEL)
        gate = jax.nn.sigmoid(jnp.dot(xn, wmg_ref[:, cols], preferred_element_type=F32) + mb_ref[:, cols])
        merged = merged + gate * jnp.dot(o_k[...], wbr_ref[br], preferred_element_type=F32)
    o_ref[...] = x + jnp.dot(merged.astype(BF16), wout_ref[...], preferred_element_type=F32)


def _const_spec(shape):
    nd = len(shape)
    return pl.BlockSpec(shape, lambda b, t: (0,) * nd, pipeline_mode=pl.Buffered(1))


def _tile_spec(width):
    return pl.BlockSpec((None, TS, width), lambda b, t: (b, t, 0))


def _call(kernel, name, x, consts, out_width, out_dtype, scratch, extra_tiles=()):
    batch, seq, _ = x.shape
    in_specs = [_tile_spec(D_MODEL)] + [_tile_spec(a.shape[-1]) for a in extra_tiles]
    in_specs += [_const_spec(c.shape) for c in consts]
    return pl.pallas_call(
        kernel,
        name=name,
        grid=(batch, seq // TS),
        in_specs=in_specs,
        out_specs=_tile_spec(out_width),
        out_shape=jax.ShapeDtypeStruct((batch, seq, out_width), out_dtype),
        scratch_shapes=scratch,
        compiler_params=pltpu.CompilerParams(
            dimension_semantics=("arbitrary", "arbitrary"), vmem_limit_bytes=VMEM_LIMIT),
    )(x, *extra_tiles, *consts)


def _retention_tables():
    pos = np.arange(CHUNK, dtype=np.float64)
    log_g = np.log1p(-np.exp2(-5.0 - np.arange(D_HEADS, dtype=np.float64)))
    rel = pos[:, None] - pos[None, :]
    dmat = np.where(rel >= 0, np.exp(log_g[:, None, None] * np.maximum(rel, 0.0)), 0.0)
    kdec = np.repeat(np.exp(log_g[None, :] * (CHUNK - 1.0 - pos)[:, None]), D_QK, axis=1)
    qdec = np.repeat(np.exp(log_g[None, :] * (pos + 1.0)[:, None]), D_V, axis=1)
    cdec = np.repeat(np.exp(log_g * CHUNK)[None, :], D_V, axis=1)
    return tuple(jnp.asarray(a, F32) for a in (dmat, kdec, qdec, cdec))


def _nsa_tables(seq):
    n_cmp = seq // CMP_STRIDE
    n_sel = seq // SEL_LEN
    n = np.arange(n_cmp) - 1
    cmp_start = n * CMP_STRIDE
    sel_start = np.arange(n_sel) * SEL_LEN
    ov = ((cmp_start[None, :] < sel_start[:, None] + SEL_LEN) & (cmp_start[None, :] + CMP_LEN > sel_start[:, None])
          & (n[None, :] >= 0) & (n[None, :] <= (seq - CMP_LEN) // CMP_STRIDE))
    gexp = np.zeros((LANES, 3 * WIDTH), np.float32)
    for lane in range(WIDTH):
        h = B_HEAD_PERM[lane // B_HEAD_DIM]
        for c in range(3):
            gexp[h * 3 + c, c * WIDTH + lane] = 1.0
    return jnp.asarray(ov, F32), jnp.asarray(gexp)


def _perm_heads(w):
    lead = w.shape[:-1]
    return w.reshape(lead + (B_HEADS, B_HEAD_DIM))[..., B_HEAD_PERM, :].reshape(lead + (WIDTH,))


def _cmp_weights(w):
    w = w.reshape(2, CMP_PHASES, B_HEAD_DIM, B_HEAD_DIM)
    eye = jnp.eye(B_GROUPS, dtype=w.dtype)
    big = jnp.einsum('spde,gh->spgdhe', w, eye).reshape(2, CMP_PHASES * B_KV, B_KV)
    return big[0].astype(BF16), big[1].astype(BF16)


def _cmp_pos(pos):
    p = jnp.tile(pos.reshape(2, CMP_PHASES, 1, B_HEAD_DIM), (1, 1, B_GROUPS, 1)).reshape(2, 1, CMP_PHASES * B_KV)
    p = jnp.concatenate([p, jnp.zeros((2, 7, CMP_PHASES * B_KV), p.dtype)], axis=1)
    return p[0], p[1]


def _block_diag(w):
    eye = jnp.eye(C_BLOCKS, dtype=w.dtype)
    return jnp.einsum('ncd,nm->ncmd', w, eye).reshape(WIDTH, WIDTH).astype(BF16)


def kernel(x, norm_g, w_in, lb_logits, a_norm_g, b_q_norm_g, b_k_norm_g, b_cmp_pos, b_cmp_wk, b_cmp_wv,
           c_conv_w, c_conv_b, c_w_ra, c_b_ra, c_w_ri, c_b_ri, c_lambda, d_norm_g, merge_b, w_branch, w_out):
    batch, seq, _ = x.shape
    depth = norm_g.shape[0]
    assert seq % TS == 0 and x.shape[-1] == D_MODEL

    p_lb = jax.nn.softmax(lb_logits.astype(F32), axis=0)
    lower_bounds = jnp.cumsum(p_lb, axis=0) - p_lb[0:1]
    dmat, kdec, qdec, cdec = _retention_tables()
    ovt, gexp = _nsa_tables(seq)
    row = lambda v: v.reshape(1, -1).astype(F32)
    vm = lambda shape, dt=F32: pltpu.VMEM(shape, dt)
    n_cmp = seq // CMP_STRIDE

    for l in range(depth):
        w = w_in[l]
        g = row(norm_g[l])
        w_a = w[:, 0:2048].astype(BF16)
        b0 = 2048
        gate_cols = jnp.pad(w[:, b0 + 1280:b0 + 1304], ((0, 0), (0, LANES - 3 * B_HEADS)))
        w_b = jnp.concatenate([_perm_heads(w[:, b0:b0 + 512]), w[:, b0 + 512:b0 + 1280], gate_cols,
                               _perm_heads(w[:, b0 + 1304:b0 + 1816])], axis=1).astype(BF16)
        c0 = b0 + 1816
        w_c = w[:, c0:c0 + 1024].astype(BF16)
        d0 = c0 + 1024
        w_d = w[:, d0:d0 + 1536].astype(BF16)
        m0 = d0 + 1536
        w_m = w[:, m0:m0 + N_BRANCH * D_MODEL].astype(BF16)

        o_a = _call(_hgrn_kernel, "hgrn", x, [g, w_a, row(lower_bounds[l]), row(a_norm_g[l])], WIDTH, BF16,
                    [vm((TS, WIDTH))] * 5 + [vm((A_HEADS, A_HEAD_DIM, A_HEAD_DIM)), vm((TS, WIDTH))])

        w1k, w2k = _cmp_weights(b_cmp_wk[l])
        w1v, w2v = _cmp_weights(b_cmp_wv[l])
        pos1, pos2 = _cmp_pos(b_cmp_pos[l])
        o_b = _call(functools.partial(_nsa_kernel, seq), "nsa", x,
                    [g, w_b, row(jnp.tile(b_q_norm_g[l], B_HEADS)), row(jnp.tile(b_k_norm_g[l], B_GROUPS)),
                     pos1, pos2, w1k, w2k, w1v, w2v, ovt, gexp], WIDTH, BF16,
                    [vm((TS, NSA_COLS)), vm((seq, B_KV), BF16), vm((seq, B_KV), BF16),
                     vm((seq + WINDOW, B_KV), BF16), vm((seq + WINDOW, B_KV), BF16),
                     vm((n_cmp, B_KV), BF16), vm((n_cmp, B_KV), BF16),
                     vm((TS // CMP_STRIDE, B_KV)), vm((TS // CMP_STRIDE, B_KV))])

        o_c = _call(_rglru_kernel, "rglru", x,
                    [g, w_c, c_conv_w[l].astype(F32), row(c_conv_b[l]), _block_diag(c_w_ra[l]), row(c_b_ra[l]),
                     _block_diag(c_w_ri[l]), row(c_b_ri[l]), row(c_lambda[l])], WIDTH, BF16,
                    [vm((TS + 2 * CONV_PAD, WIDTH)), vm((8, WIDTH))])

        o_d = _call(_ret_kernel, "retention", x, [g, w_d, dmat, kdec, qdec, cdec, row(d_norm_g[l])], WIDTH, BF16,
                    [vm((TS, D_HEADS * D_QK)), vm((TS, D_HEADS * D_QK)), vm((TS, WIDTH)), vm((TS, WIDTH)),
                     vm((D_HEADS, LANES, D_V)), vm((TS, WIDTH))])

        wbr = w_branch[l]
        wbr = jnp.stack([wbr[0], _perm_heads(wbr[1].T).T, wbr[2], wbr[3]], axis=0).astype(BF16)
        x = _call(_merge_kernel, "merge", x, [g, w_m, row(merge_b[l]), wbr, w_out[l].astype(BF16)],
                  D_MODEL, x.dtype, [], extra_tiles=(o_a, o_b, o_c, o_d))
    return x
```

```python
import functools

import numpy as np
import jax
import jax.numpy as jnp
from jax import lax
from jax.experimental import pallas as pl
from jax.experimental.pallas import tpu as pltpu

F32 = jnp.float32
BF16 = jnp.bfloat16

D_MODEL = 1024
WIDTH = 512
EPS = 1e-6
NEG_INF = -1e30
BIG = 1e30

A_HEADS = 4
A_HEAD_DIM = 128
CHUNK = 64

B_HEADS = 8
B_GROUPS = 2
B_REP = B_HEADS // B_GROUPS
B_HEAD_DIM = 64
B_KV = B_GROUPS * B_HEAD_DIM
CMP_LEN = 32
CMP_STRIDE = 16
SEL_LEN = 64
SEL_TOPN = 8
WINDOW = 256
Q_BLOCK = 128

C_BLOCKS = 8
C_BLOCK_DIM = 64
CONV_WIDTH = 4
RG_C = 8.0

D_HEADS = 4
D_QK = 64
D_V = 128

N_BRANCH = 4

TS = 512
LANES = 128
VMEM_LIMIT = 56 * 1024 * 1024
LOG2E = 1.4426950408889634

B_HEAD_PERM = (0, 4, 1, 5, 2, 6, 3, 7)
B_SLOPES = tuple(float(2.0 ** (-8.0 * (h + 1) / B_HEADS)) for h in range(B_HEADS))


def _dot(a, b):
    return jnp.dot(a.astype(BF16), b.astype(BF16), preferred_element_type=F32)


def _dot_nt(a, b):
    return lax.dot_general(a.astype(BF16), b.astype(BF16), (((1,), (1,)), ((), ())),
                           preferred_element_type=F32)


def _dot_tn(a, b):
    return lax.dot_general(a.astype(BF16), b.astype(BF16), (((0,), (0,)), ((), ())),
                           preferred_element_type=F32)


def _rms_rows(x, g):
    ms = jnp.mean(x * x, axis=-1, keepdims=True)
    return x * lax.rsqrt(ms + EPS) * g


def _silu(z):
    return z * jax.nn.sigmoid(z)


def _iota(shape, dim):
    return lax.broadcasted_iota(jnp.int32, shape, dim)


def _normed_input(x_ref, g_ref):
    return _rms_rows(x_ref[...], g_ref[...]).astype(BF16)


N_CHUNKS = TS // CHUNK
INTRA = 256


def _chunk_broadcast(v, row):
    v3 = v.reshape(N_CHUNKS, CHUNK, v.shape[-1])
    return jnp.broadcast_to(v3[:, row:row + 1, :], v3.shape).reshape(v.shape)


def _chunk_rows(v, row):
    return v.reshape(N_CHUNKS, CHUNK, v.shape[-1])[:, row, :]


def _chunk_block_diag(v):
    zero = jnp.zeros((CHUNK, v.shape[-1]), v.dtype)
    return jnp.concatenate(
        [jnp.concatenate([zero] * c + [v[c * CHUNK:(c + 1) * CHUNK]] + [zero] * (N_CHUNKS - 1 - c), axis=1)
         for c in range(N_CHUNKS)], axis=0)


def _chunk_diag_blocks(v, width):
    return jnp.concatenate([v[c * CHUNK:(c + 1) * CHUNK, c * width:(c + 1) * width] for c in range(N_CHUNKS)], axis=0)


def _hgrn_kernel(x_ref, g_ref, w_ref, lb_ref, ng_ref, o_ref, st_s):
    @pl.when(pl.program_id(1) == 0)
    def _():
        st_s[...] = jnp.zeros_like(st_s)

    u = jnp.dot(_normed_input(x_ref, g_ref), w_ref[...], preferred_element_type=F32)
    q = u[:, 0:WIDTH]
    fl = u[:, WIDTH:2 * WIDTH]
    lb = lb_ref[...]
    f = lb + (1.0 - lb) * jax.nn.sigmoid(fl)
    k = (1.0 - lb) * jax.nn.sigmoid(-fl)
    vb = u[:, 2 * WIDTH:3 * WIDTH].astype(BF16)
    b = jnp.log(f)
    row_in_chunk = _iota((TS, WIDTH), 0) & (CHUNK - 1)
    s = 1
    while s < CHUNK:
        b = b + jnp.where(row_in_chunk >= s, pltpu.roll(b, s, 0), 0.0)
        s *= 2
    b_mid = _chunk_broadcast(b, CHUNK // 2 - 1)
    b_end = _chunk_broadcast(b, CHUNK - 1)
    qe = (q * jnp.exp(b - b_mid)).astype(BF16)
    ke = (k * jnp.exp(b_mid - b)).astype(BF16)
    k_end = (k * jnp.exp(b_end - b)).astype(BF16)
    qb = (q * jnp.exp(b)).astype(BF16)
    dec = jnp.exp(_chunk_rows(b, CHUNK - 1))

    ri, ci = _iota((INTRA, INTRA), 0), _iota((INTRA, INTRA), 1)
    block_causal = jnp.logical_and(ri >= ci, (ri >> 6) == (ci >> 6))
    ng = ng_ref[...]
    for h in range(A_HEADS):
        sl = slice(h * A_HEAD_DIM, (h + 1) * A_HEAD_DIM)
        intra = []
        for i in range(TS // INTRA):
            rows = slice(i * INTRA, (i + 1) * INTRA)
            sc = jnp.where(block_causal, _dot_nt(qe[rows, sl], ke[rows, sl]), 0.0)
            intra.append(_dot(sc, vb[rows, sl]))
        kv_all = _dot_tn(vb[:, sl], _chunk_block_diag(k_end[:, sl]))
        st = st_s[h]
        states = []
        for c in range(N_CHUNKS):
            states.append(st.astype(BF16))
            st = st * dec[c:c + 1, sl] + kv_all[:, c * A_HEAD_DIM:(c + 1) * A_HEAD_DIM]
        st_s[h] = st
        inter = _chunk_diag_blocks(_dot_nt(qb[:, sl], jnp.concatenate(states, axis=0)), A_HEAD_DIM)
        o = jnp.concatenate(intra, axis=0) + inter
        z = u[:, 3 * WIDTH + h * A_HEAD_DIM:3 * WIDTH + (h + 1) * A_HEAD_DIM]
        o_ref[:, sl] = (_rms_rows(o, ng[:, sl]) * _silu(z)).astype(o_ref.dtype)


def _ret_kernel(x_ref, g_ref, w_ref, dmat_ref, kdec_ref, qdec_ref, cdec_ref, ng_ref, o_ref, st_s):
    @pl.when(pl.program_id(1) == 0)
    def _():
        st_s[...] = jnp.zeros_like(st_s)

    u = jnp.dot(_normed_input(x_ref, g_ref), w_ref[...], preferred_element_type=F32)
    nqk = D_HEADS * D_QK
    q = u[:, 0:nqk] * (D_QK ** -0.5)
    k = u[:, nqk:2 * nqk]
    kb = k.astype(BF16)
    kd = (k * kdec_ref[...]).astype(BF16)
    vb = u[:, 2 * nqk:2 * nqk + WIDTH].astype(BF16)
    first_half = _iota((TS, LANES), 1) < D_QK
    qdec = qdec_ref[...]
    cdec = cdec_ref[...]
    ng = ng_ref[...]
    for h in range(D_HEADS):
        pair = slice((h // 2) * LANES, (h // 2 + 1) * LANES)
        vsl = slice(h * D_V, (h + 1) * D_V)
        qm = jnp.where(first_half if h % 2 == 0 else jnp.logical_not(first_half), q[:, pair], 0.0).astype(BF16)
        intra = []
        for i in range(TS // INTRA):
            rows = slice(i * INTRA, (i + 1) * INTRA)
            sc = _dot_nt(qm[rows], kb[rows, pair]) * dmat_ref[h]
            intra.append(_dot(sc, vb[rows, vsl]))
        kv_all = _dot_tn(kd[:, pair], _chunk_block_diag(vb[:, vsl]))
        st = st_s[h]
        states = []
        for c in range(N_CHUNKS):
            states.append(st.astype(BF16))
            st = st * cdec[:, vsl] + kv_all[:, c * D_V:(c + 1) * D_V]
        st_s[h] = st
        inter = _chunk_diag_blocks(_dot(qm, jnp.concatenate(states, axis=1)), D_V)
        o = jnp.concatenate(intra, axis=0) + inter * qdec[:, vsl]
        oc = o - jnp.mean(o, axis=-1, keepdims=True)
        var = jnp.mean(oc * oc, axis=-1, keepdims=True)
        z = u[:, 2 * nqk + WIDTH + h * D_V:2 * nqk + WIDTH + (h + 1) * D_V]
        o_ref[:, vsl] = (oc * lax.rsqrt(var + EPS) * ng[:, vsl] * _silu(z)).astype(o_ref.dtype)


CONV_PAD = 8
GROUP = 8


def _expand_groups(v):
    n = v.shape[0]
    sel = jnp.where((_iota((n * GROUP, n), 0) >> 3) == _iota((n * GROUP, n), 1), 1.0, 0.0).astype(BF16)
    hi = v.astype(BF16)
    r1 = v - hi.astype(F32)
    mid = r1.astype(BF16)
    lo = (r1 - mid.astype(F32)).astype(BF16)
    return (jnp.dot(sel, hi, preferred_element_type=F32) + jnp.dot(sel, mid, preferred_element_type=F32)
            + jnp.dot(sel, lo, preferred_element_type=F32))


def _scan_steps(a, b, row, length, axis=0):
    s = 1
    while s < length:
        keep = row >= s
        a_prev = jnp.where(keep, pltpu.roll(a, s, axis), 1.0)
        b_prev = jnp.where(keep, pltpu.roll(b, s, axis), 0.0)
        b = a * b_prev + b
        a = a * a_prev
        s *= 2
    return a, b


def _rglru_kernel(x_ref, g_ref, w_ref, cw_ref, cb_ref, wra_ref, bra_ref, wri_ref, bri_ref, lam_ref,
                  o_ref, ext_s, h_s, a_s, b_s):
    t = pl.program_id(1)

    @pl.when(t == 0)
    def _():
        ext_s[0:CONV_PAD, :] = jnp.zeros((CONV_PAD, WIDTH), F32)
        h_s[...] = jnp.zeros_like(h_s)

    @pl.when(t > 0)
    def _():
        ext_s[0:CONV_PAD, :] = ext_s[TS:TS + CONV_PAD, :]

    u = jnp.dot(_normed_input(x_ref, g_ref), w_ref[...], preferred_element_type=F32)
    ext_s[CONV_PAD:CONV_PAD + TS, :] = u[:, 0:WIDTH]
    cz = u[:, WIDTH:2 * WIDTH]
    cw = cw_ref[...]
    xc = cb_ref[...] + cw[CONV_WIDTH - 1:CONV_WIDTH, :] * u[:, 0:WIDTH]
    for j in range(CONV_WIDTH - 1):
        back = CONV_WIDTH - 1 - j
        xc = xc + cw[j:j + 1, :] * ext_s[CONV_PAD - back:CONV_PAD - back + TS, :]
    r = jax.nn.sigmoid(_dot(xc, wra_ref[...]) + bra_ref[...])
    ig = jax.nn.sigmoid(_dot(xc, wri_ref[...]) + bri_ref[...])
    neg_lam = -lam_ref[...]
    softplus = jnp.maximum(neg_lam, 0.0) + jnp.log1p(jnp.exp(-jnp.abs(neg_lam)))
    log_a = -RG_C * r * softplus
    a = jnp.exp(log_a)
    th = jnp.tanh(log_a)
    b = jnp.sqrt(-2.0 * th / (1.0 - th)) * (ig * xc)
    n_groups = TS // GROUP
    grouped = (n_groups, GROUP, WIDTH)
    a, b = _scan_steps(a.reshape(grouped), b.reshape(grouped), _iota(grouped, 1), GROUP, axis=1)
    a, b = a.reshape(TS, WIDTH), b.reshape(TS, WIDTH)
    last_rows = pl.ds(GROUP - 1, n_groups, stride=GROUP)
    for j in range(WIDTH // LANES):
        a_s[j] = a[:, j * LANES:(j + 1) * LANES]
        b_s[j] = b[:, j * LANES:(j + 1) * LANES]
    a_last = jnp.concatenate([a_s[j, last_rows, :] for j in range(WIDTH // LANES)], axis=1)
    b_last = jnp.concatenate([b_s[j, last_rows, :] for j in range(WIDTH // LANES)], axis=1)
    group = _iota((n_groups, WIDTH), 0)
    a_last, b_last = _scan_steps(a_last, b_last, group, n_groups)
    h0 = h_s[0:1, :]
    h_after = a_last * h0 + b_last
    h_before = jnp.where(group == 0, h0, pltpu.roll(h_after, 1, 0))
    h = a * _expand_groups(h_before) + b
    h_s[0:1, :] = h_after[n_groups - 1:n_groups, :]
    o_ref[...] = (h * _silu(cz)).astype(o_ref.dtype)


NSA_Q = 0
NSA_KC = 512
NSA_VC = 640
NSA_KS = 768
NSA_VS = 896
NSA_KW = 1024
NSA_VW = 1152
NSA_G = 1280
NSA_Z = 1408
NSA_COLS = 1920
SEL_KEYS = TS
WIN_KEYS = Q_BLOCK + WINDOW
CMP_PHASES = CMP_STRIDE
AUG = 2 * LANES
POS_ROWS = 16
MASK_BIG = float(2.0 ** 40)
SUM_ROWS = 16


def _half_rms(v, first_half):
    v2 = v * v
    s0 = jnp.sum(jnp.where(first_half, v2, 0.0), axis=-1, keepdims=True)
    s1 = jnp.sum(jnp.where(first_half, 0.0, v2), axis=-1, keepdims=True)
    ms = jnp.where(first_half, s0, s1) * (1.0 / B_HEAD_DIM)
    return v * lax.rsqrt(ms + EPS)


def _softmax_cols(s):
    p = jnp.exp2(s - jnp.max(s, axis=0, keepdims=True))
    return p, jnp.sum(p, axis=0, keepdims=True)


def _nsa_kernel(seq, x_ref, g_ref, w_ref, qg_ref, kg_ref, pos1_ref, pos2_ref, w1k_ref, w2k_ref,
                w1v_ref, w2v_ref, ov_ref, slope_ref, possel_ref, poswin_ref, poscmp_ref, o_ref,
                u_s, ks_c, vst_c, kw_c, vwt_c, kc_c, vc_c, p1k_s, p1v_s):
    n_cmp = seq // CMP_STRIDE
    n_sel = seq // SEL_LEN
    ti = pl.program_id(1)

    @pl.when(ti == 0)
    def _():
        for ref in (ks_c, vst_c, kw_c, vwt_c, kc_c, vc_c, p1k_s, p1v_s):
            ref[...] = jnp.zeros_like(ref)

    u_s[...] = jnp.dot(_normed_input(x_ref, g_ref), w_ref[...], preferred_element_type=F32)
    row0 = pl.multiple_of(ti * TS, TS)
    kg = kg_ref[...]
    half_t = _iota((TS, LANES), 1) < B_HEAD_DIM

    ks_c[pl.ds(row0, TS), :] = (_half_rms(u_s[:, NSA_KS:NSA_KS + B_KV], half_t) * kg).astype(BF16)
    vst_c[ti] = u_s[:, NSA_VS:NSA_VS + B_KV].T.astype(BF16)
    kw_c[pl.ds(row0 + WINDOW, TS), :] = (_half_rms(u_s[:, NSA_KW:NSA_KW + B_KV], half_t) * kg).astype(BF16)
    for i in range(TS // Q_BLOCK):
        vwt_c[ti * (TS // Q_BLOCK) + WINDOW // Q_BLOCK + i] = (
            u_s[i * Q_BLOCK:(i + 1) * Q_BLOCK, NSA_VW:NSA_VW + B_KV].T.astype(BF16))

    n_grp = TS // CMP_STRIDE
    wide = CMP_PHASES * B_KV
    phase_match = (_iota((TS, wide), 0) & (CMP_STRIDE - 1)) == (_iota((TS, wide), 1) >> 7)
    pool = jnp.where((_iota((n_grp, TS), 1) >> 4) == _iota((n_grp, TS), 0), 1.0, 0.0).astype(BF16)
    first_row = _iota((n_grp, B_KV), 0) == 0
    half_g = _iota((n_grp, LANES), 1) < B_HEAD_DIM
    crow = pl.ds(pl.multiple_of(ti * n_grp, n_grp), n_grp)
    for col, w1_ref, w2_ref, p1_s, cache, is_key in ((NSA_KC, w1k_ref, w2k_ref, p1k_s, kc_c, True),
                                                      (NSA_VC, w1v_ref, w2v_ref, p1v_s, vc_c, False)):
        src = u_s[:, col:col + B_KV]
        spread = jnp.where(phase_match, jnp.concatenate([src] * CMP_PHASES, axis=1), 0.0).astype(BF16)
        grouped = jnp.dot(pool, spread, preferred_element_type=F32)
        c1 = _dot(pos1_ref[...], w1_ref[...])[0:1, :]
        c2 = _dot(pos2_ref[...], w2_ref[...])[0:1, :]
        p1 = _dot(grouped, w1_ref[...]) + c1
        p2 = _dot(grouped, w2_ref[...]) + c2
        p1_prev = jnp.where(first_row, p1_s[n_grp - 1:n_grp, :], pltpu.roll(p1, 1, 0))
        p1_s[...] = p1
        blk = p1_prev + p2
        if is_key:
            blk = _half_rms(blk, half_g) * kg
        cache[crow, :] = blk.astype(BF16)

    half_q = _iota((Q_BLOCK, LANES), 1) < B_HEAD_DIM
    qg = qg_ref[...]
    n_q = B_REP * Q_BLOCK
    q_rows_first = _iota((LANES, n_q), 0) < B_HEAD_DIM
    o_rows_first = _iota((LANES, Q_BLOCK), 0) < B_HEAD_DIM

    def tile4(a):
        return jnp.concatenate([a] * B_REP, axis=1)

    def sub_block(sb, carry):
        r0 = pl.multiple_of(sb * Q_BLOCK, Q_BLOCK)
        start = row0 + r0
        q_t = []
        for r in range(B_REP):
            qt = u_s[pl.ds(r0, Q_BLOCK), NSA_Q + r * LANES:NSA_Q + (r + 1) * LANES]
            qn = _half_rms(qt, half_q) * qg[:, r * LANES:(r + 1) * LANES] * (B_HEAD_DIM ** -0.5 * LOG2E)
            q_t.append(qn.T)
        q_t = jnp.concatenate(q_t, axis=1)
        base = [jnp.concatenate([jnp.where(q_rows_first, q_t, 0.0).astype(BF16), slope_ref[0]], axis=0),
                jnp.concatenate([jnp.where(q_rows_first, 0.0, q_t).astype(BF16), slope_ref[1]], axis=0)]
        n_base = LANES + POS_ROWS

        t_c = start + _iota((n_cmp, Q_BLOCK), 1)
        nprime = _iota((n_cmp, Q_BLOCK), 0)
        madd_c = tile4(jnp.where(t_c >= nprime * CMP_STRIDE + (CMP_STRIDE - 1),
                                 jnp.where(nprime >= 1, 0.0, NEG_INF), NEG_INF))
        any_c = jnp.where(start + (_iota((1, n_q), 1) & (Q_BLOCK - 1)) >= CMP_LEN - 1, 1.0, 0.0)
        kaug_c = jnp.concatenate([kc_c[...], poscmp_ref[...]], axis=1)
        vct = vc_c[...].astype(F32).T.astype(BF16)
        t_s = start + _iota((n_sel, Q_BLOCK), 1)
        blk_t = t_s >> 6
        jrow = _iota((n_sel, Q_BLOCK), 0)
        o_cmp, q_aug = [], []
        for g in range(B_GROUPS):
            q_c = jnp.concatenate([base[g], jnp.zeros((AUG - n_base, n_q), BF16)], axis=0)
            p, l = _softmax_cols(jnp.dot(kaug_c, q_c, preferred_element_type=F32) + madd_c)
            p = p * (any_c / l)
            o_cmp.append(_dot(vct, p))
            p_sum = (p[:, 0:Q_BLOCK] + p[:, Q_BLOCK:2 * Q_BLOCK]
                     + p[:, 2 * Q_BLOCK:3 * Q_BLOCK] + p[:, 3 * Q_BLOCK:4 * Q_BLOCK])
            imp = jnp.dot(ov_ref[...], p_sum, precision=lax.Precision.HIGHEST,
                          preferred_element_type=F32)
            imp = jnp.where(jrow == blk_t, BIG, jnp.where(jrow < blk_t, imp, -BIG))
            cnt = jnp.zeros((n_sel, Q_BLOCK), F32)
            for jp in range(n_sel):
                other = imp[jp:jp + 1, :]
                ge = jnp.where(other >= imp, 1.0, 0.0)
                gt = jnp.where(other > imp, 1.0, 0.0)
                cnt = cnt + jnp.where(jrow > jp, ge, gt)
            sel_neg = jnp.where(cnt < SEL_TOPN, 0.0, -MASK_BIG).astype(BF16)
            q_aug.append(jnp.concatenate([base[g], tile4(sel_neg),
                                          jnp.zeros((AUG - n_base - n_sel, n_q), BF16)], axis=0))

        def sel_step(kc, st, diagonal):
            k0 = pl.multiple_of(kc * SEL_KEYS, SEL_KEYS)
            kaug = jnp.concatenate([ks_c[pl.ds(k0, SEL_KEYS), :], possel_ref[pl.ds(k0, SEL_KEYS), :]], axis=1)
            v_aug = jnp.concatenate([vst_c[kc], jnp.ones((SUM_ROWS, SEL_KEYS), BF16)], axis=0)
            scores = [jnp.dot(kaug, q_aug[g], preferred_element_type=F32) for g in range(B_GROUPS)]
            if diagonal:
                keypos = k0 + _iota((SEL_KEYS, Q_BLOCK), 0)
                madd = tile4(jnp.where(keypos <= start + _iota((SEL_KEYS, Q_BLOCK), 1), 0.0, NEG_INF))
                scores = [s + madd for s in scores]
            m_new = [jnp.maximum(st[2 * g], jnp.max(scores[g], axis=0, keepdims=True)) for g in range(B_GROUPS)]
            probs = [jnp.exp2((scores[g] - m_new[g]).astype(BF16)) for g in range(B_GROUPS)]
            out = []
            for g in range(B_GROUPS):
                out += [m_new[g], jnp.exp2(st[2 * g] - m_new[g]) * st[2 * g + 1]
                        + jnp.dot(v_aug, probs[g], preferred_element_type=F32)]
            return tuple(out)

        init = []
        for g in range(B_GROUPS):
            init += [jnp.full((1, n_q), NEG_INF, F32), jnp.zeros((LANES + SUM_ROWS, n_q), F32)]
        st = lax.fori_loop(0, ti, functools.partial(sel_step, diagonal=False), tuple(init))
        st = sel_step(ti, st, diagonal=True)
        o_sel = [st[2 * g + 1][0:LANES] / st[2 * g + 1][LANES:LANES + 1] for g in range(B_GROUPS)]

        kaug_w = jnp.concatenate([kw_c[pl.ds(start, WIN_KEYS), :], poswin_ref[pl.ds(start, WIN_KEYS), :]], axis=1)
        wblk = ti * (TS // Q_BLOCK) + sb
        vwt = jnp.concatenate([vwt_c[wblk + i] for i in range(WIN_KEYS // Q_BLOCK)], axis=1)
        spos = (start - WINDOW) + _iota((WIN_KEYS, Q_BLOCK), 0)
        dist_w = (start + _iota((WIN_KEYS, Q_BLOCK), 1)) - spos
        madd_w = tile4(jnp.where(dist_w >= 0, jnp.where(dist_w < WINDOW, jnp.where(spos >= 0, 0.0, NEG_INF),
                                                        NEG_INF), NEG_INF))
        vwt = jnp.concatenate([vwt, jnp.ones((SUM_ROWS, WIN_KEYS), BF16)], axis=0)
        o_win = []
        for g in range(B_GROUPS):
            s = jnp.dot(kaug_w, q_aug[g], preferred_element_type=F32) + madd_w
            p = jnp.exp2((s - jnp.max(s, axis=0, keepdims=True)).astype(BF16))
            pv = jnp.dot(vwt, p, preferred_element_type=F32)
            o_win.append(pv[0:LANES] / pv[LANES:LANES + 1])

        gates_t = jax.nn.sigmoid(u_s[pl.ds(r0, Q_BLOCK), NSA_G:NSA_G + LANES]).T
        for r in range(B_REP):
            cs = slice(r * Q_BLOCK, (r + 1) * Q_BLOCK)
            acc = jnp.zeros((LANES, Q_BLOCK), F32)
            for c, branch in enumerate((o_cmp, o_sel, o_win)):
                both = jnp.where(o_rows_first, branch[0][:, cs] * gates_t[r * 3 + c:r * 3 + c + 1, :],
                                 branch[1][:, cs] * gates_t[(B_REP + r) * 3 + c:(B_REP + r) * 3 + c + 1, :])
                acc = acc + both
            z = u_s[pl.ds(r0, Q_BLOCK), NSA_Z + r * LANES:NSA_Z + (r + 1) * LANES]
            o_ref[pl.ds(r0, Q_BLOCK), r * LANES:(r + 1) * LANES] = (acc.T * _silu(z)).astype(o_ref.dtype)
        return carry

    lax.fori_loop(0, TS // Q_BLOCK, sub_block, 0)


def _merge_kernel(x_ref, oa_ref, ob_ref, oc_ref, od_ref, g_ref, wmg_ref, mb_ref, wbr_ref, wout_ref, o_ref):
    x = x_ref[...]
    xn = _rms_rows(x, g_ref[...]).astype(BF16)
    merged = jnp.zeros((TS, D_MODEL), F32)
    for br, o_k in enumerate((oa_ref, ob_ref, oc_ref, od_ref)):
        cols = slice(br * D_MODEL, (br + 1) * D_MODEL)
        gate = jax.nn.sigmoid(jnp.dot(xn, wmg_ref[:, cols], preferred_element_type=F32) + mb_ref[:, cols])
        merged = merged + gate * jnp.dot(o_k[...], wbr_ref[br], preferred_element_type=F32)
    o_ref[...] = x + jnp.dot(merged.astype(BF16), wout_ref[...], preferred_element_type=F32)


def _const_spec(shape):
    nd = len(shape)
    return pl.BlockSpec(shape, lambda b, t: (0,) * nd, pipeline_mode=pl.Buffered(1))


def _tile_spec(nb, width):
    return pl.BlockSpec((nb, TS, width), lambda b, t: (b, t, 0))


def _per_batch(body, n_tiles, n_consts, nb):
    def kernel(*refs):
        tiles, consts = refs[:n_tiles], refs[n_tiles:n_tiles + n_consts]
        out, scratch = refs[n_tiles + n_consts], refs[n_tiles + n_consts + 1:]
        for i in range(nb):
            body(*[t.at[i] for t in tiles], *consts, out.at[i], *[s.at[i] for s in scratch])
    return kernel


def _call(body, name, x, consts, out_width, out_dtype, scratch, extra_tiles=(), nb=1):
    batch, seq, _ = x.shape
    assert batch % nb == 0
    in_specs = [_tile_spec(nb, D_MODEL)] + [_tile_spec(nb, a.shape[-1]) for a in extra_tiles]
    in_specs += [_const_spec(c.shape) for c in consts]
    scratch = [pltpu.VMEM((nb,) + tuple(s.shape), s.dtype) for s in scratch]
    return pl.pallas_call(
        _per_batch(body, 1 + len(extra_tiles), len(consts), nb),
        name=name,
        grid=(batch // nb, seq // TS),
        in_specs=in_specs,
        out_specs=_tile_spec(nb, out_width),
        out_shape=jax.ShapeDtypeStruct((batch, seq, out_width), out_dtype),
        scratch_shapes=scratch,
        compiler_params=pltpu.CompilerParams(
            dimension_semantics=("arbitrary", "arbitrary"), vmem_limit_bytes=VMEM_LIMIT),
    )(x, *extra_tiles, *consts)


def _retention_tables():
    pos = np.arange(TS, dtype=np.float64)
    log_g = np.log1p(-np.exp2(-5.0 - np.arange(D_HEADS, dtype=np.float64)))
    rel = pos[:, None] - pos[None, :]
    same = (pos[:, None] // CHUNK) == (pos[None, :] // CHUNK)
    dmat = np.where((rel >= 0) & same, np.exp(log_g[:, None, None] * np.maximum(rel, 0.0)), 0.0)
    dmat = dmat[:, :INTRA, :INTRA]
    inpos = pos % CHUNK
    kdec = np.repeat(np.exp(log_g[None, :] * (CHUNK - 1.0 - inpos)[:, None]), D_QK, axis=1)
    qdec = np.repeat(np.exp(log_g[None, :] * (inpos + 1.0)[:, None]), D_V, axis=1)
    cdec = np.repeat(np.exp(log_g * CHUNK)[None, :], D_V, axis=1)
    return tuple(jnp.asarray(a, F32) for a in (dmat, kdec, qdec, cdec))


def _bf16_pieces(v, n):
    out = []
    rem = np.asarray(v, np.float64)
    for _ in range(n):
        piece = rem.astype(BF16).astype(np.float64)
        out.append(piece)
        rem = rem - piece
    return out


def _position_tile(pos, block_ids=None):
    pos = np.maximum(np.asarray(pos), 0)
    tab = np.zeros((pos.shape[0], LANES), np.float32)
    for i in range(3):
        tab[:, 2 * i] = (pos // 256) * 256
        tab[:, 2 * i + 1] = pos % 256
    if block_ids is not None:
        tab[np.arange(pos.shape[0]), POS_ROWS + np.asarray(block_ids)] = 1.0
    return jnp.asarray(tab, BF16)


def _nsa_tables(seq):
    n_cmp = seq // CMP_STRIDE
    n_sel = seq // SEL_LEN
    n = np.arange(n_cmp) - 1
    cmp_start = n * CMP_STRIDE
    sel_start = np.arange(n_sel) * SEL_LEN
    ov = ((cmp_start[None, :] < sel_start[:, None] + SEL_LEN) & (cmp_start[None, :] + CMP_LEN > sel_start[:, None])
          & (n[None, :] >= 0) & (n[None, :] <= (seq - CMP_LEN) // CMP_STRIDE))
    slope = np.zeros((B_GROUPS, POS_ROWS, B_REP * Q_BLOCK), np.float32)
    for g in range(B_GROUPS):
        for r in range(B_REP):
            pieces = _bf16_pieces(B_SLOPES[g * B_REP + r] * LOG2E, 3)
            for i, piece in enumerate(pieces):
                slope[g, 2 * i:2 * i + 2, r * Q_BLOCK:(r + 1) * Q_BLOCK] = piece
    tpos = np.arange(seq)
    possel = _position_tile(tpos, tpos // SEL_LEN)
    poswin = _position_tile(np.arange(seq + WINDOW) - WINDOW)
    poscmp = _position_tile(np.arange(n_cmp) * CMP_STRIDE + (CMP_STRIDE - 1))
    return (jnp.asarray(ov, F32), jnp.asarray(slope, BF16), possel, poswin, poscmp)


def _perm_heads(w):
    lead = w.shape[:-1]
    return w.reshape(lead + (B_HEADS, B_HEAD_DIM))[..., B_HEAD_PERM, :].reshape(lead + (WIDTH,))


def _cmp_weights(w):
    w = w.reshape(2, CMP_PHASES, B_HEAD_DIM, B_HEAD_DIM)
    eye = jnp.eye(B_GROUPS, dtype=w.dtype)
    big = jnp.einsum('spde,gh->spgdhe', w, eye).reshape(2, CMP_PHASES * B_KV, B_KV)
    return big[0].astype(BF16), big[1].astype(BF16)


def _cmp_pos(pos):
    p = jnp.tile(pos.reshape(2, CMP_PHASES, 1, B_HEAD_DIM), (1, 1, B_GROUPS, 1)).reshape(2, 1, CMP_PHASES * B_KV)
    p = jnp.concatenate([p, jnp.zeros((2, 7, CMP_PHASES * B_KV), p.dtype)], axis=1)
    return p[0], p[1]


def _block_diag(w):
    eye = jnp.eye(C_BLOCKS, dtype=w.dtype)
    return jnp.einsum('ncd,nm->ncmd', w, eye).reshape(WIDTH, WIDTH).astype(BF16)


def kernel(x, norm_g, w_in, lb_logits, a_norm_g, b_q_norm_g, b_k_norm_g, b_cmp_pos, b_cmp_wk, b_cmp_wv,
           c_conv_w, c_conv_b, c_w_ra, c_b_ra, c_w_ri, c_b_ri, c_lambda, d_norm_g, merge_b, w_branch, w_out):
    batch, seq, _ = x.shape
    depth = norm_g.shape[0]
    assert seq % TS == 0 and x.shape[-1] == D_MODEL

    p_lb = jax.nn.softmax(lb_logits.astype(F32), axis=0)
    lower_bounds = jnp.cumsum(p_lb, axis=0) - p_lb[0:1]
    dmat, kdec, qdec, cdec = _retention_tables()
    nsa_tabs = _nsa_tables(seq)
    row = lambda v: v.reshape(1, -1).astype(F32)
    vm = lambda shape, dt=F32: pltpu.VMEM(shape, dt)
    n_cmp = seq // CMP_STRIDE

    for l in range(depth):
        w = w_in[l]
        g = row(norm_g[l])
        w_a = w[:, 0:2048].astype(BF16)
        b0 = 2048
        gate_cols = jnp.pad(w[:, b0 + 1280:b0 + 1304], ((0, 0), (0, LANES - 3 * B_HEADS)))
        w_b = jnp.concatenate([_perm_heads(w[:, b0:b0 + 512]), w[:, b0 + 512:b0 + 1280], gate_cols,
                               _perm_heads(w[:, b0 + 1304:b0 + 1816])], axis=1).astype(BF16)
        c0 = b0 + 1816
        w_c = w[:, c0:c0 + 1024].astype(BF16)
        d0 = c0 + 1024
        w_d = w[:, d0:d0 + 1536].astype(BF16)
        m0 = d0 + 1536
        w_m = w[:, m0:m0 + N_BRANCH * D_MODEL].astype(BF16)

        o_a = _call(_hgrn_kernel, "hgrn", x, [g, w_a, row(lower_bounds[l]), row(a_norm_g[l])], WIDTH, BF16,
                    [vm((A_HEADS, A_HEAD_DIM, A_HEAD_DIM))])

        w1k, w2k = _cmp_weights(b_cmp_wk[l])
        w1v, w2v = _cmp_weights(b_cmp_wv[l])
        pos1, pos2 = _cmp_pos(b_cmp_pos[l])
        o_b = _call(functools.partial(_nsa_kernel, seq), "nsa", x,
                    [g, w_b, row(jnp.tile(b_q_norm_g[l], B_HEADS)), row(jnp.tile(b_k_norm_g[l], B_GROUPS)),
                     pos1, pos2, w1k, w2k, w1v, w2v, *nsa_tabs], WIDTH, BF16,
                    [vm((TS, NSA_COLS)), vm((seq, B_KV), BF16), vm((seq // TS, B_KV, TS), BF16),
                     vm((seq + WINDOW, B_KV), BF16), vm(((seq + WINDOW) // Q_BLOCK, B_KV, Q_BLOCK), BF16),
                     vm((n_cmp, B_KV), BF16), vm((n_cmp, B_KV), BF16),
                     vm((TS // CMP_STRIDE, B_KV)), vm((TS // CMP_STRIDE, B_KV))])

        o_c = _call(_rglru_kernel, "rglru", x,
                    [g, w_c, c_conv_w[l].astype(F32), row(c_conv_b[l]), _block_diag(c_w_ra[l]), row(c_b_ra[l]),
                     _block_diag(c_w_ri[l]), row(c_b_ri[l]), row(c_lambda[l])], WIDTH, BF16,
                    [vm((TS + 2 * CONV_PAD, WIDTH)), vm((8, WIDTH)), vm((WIDTH // LANES, TS, LANES)),
                     vm((WIDTH // LANES, TS, LANES))])

        o_d = _call(_ret_kernel, "retention", x, [g, w_d, dmat, kdec, qdec, cdec, row(d_norm_g[l])], WIDTH, BF16,
                    [vm((D_HEADS, LANES, D_V))])

        wbr = w_branch[l]
        wbr = jnp.stack([wbr[0], _perm_heads(wbr[1].T).T, wbr[2], wbr[3]], axis=0).astype(BF16)
        x = _call(_merge_kernel, "merge", x, [g, w_m, row(merge_b[l]), wbr, w_out[l].astype(BF16)],
                  D_MODEL, x.dtype, [], extra_tiles=(o_a, o_b, o_c, o_d))
    return x
```

```python
import functools

import numpy as np
import jax
import jax.numpy as jnp
from jax import lax
from jax.experimental import pallas as pl
from jax.experimental.pallas import tpu as pltpu

F32 = jnp.float32
BF16 = jnp.bfloat16

D_MODEL = 1024
WIDTH = 512
EPS = 1e-6
NEG_INF = -1e30
BIG = 1e30

A_HEADS = 4
A_HEAD_DIM = 128
CHUNK = 64

B_HEADS = 8
B_GROUPS = 2
B_REP = B_HEADS // B_GROUPS
B_HEAD_DIM = 64
B_KV = B_GROUPS * B_HEAD_DIM
CMP_LEN = 32
CMP_STRIDE = 16
SEL_LEN = 64
SEL_TOPN = 8
WINDOW = 256
Q_BLOCK = 128

C_BLOCKS = 8
C_BLOCK_DIM = 64
CONV_WIDTH = 4
RG_C = 8.0

D_HEADS = 4
D_QK = 64
D_V = 128

N_BRANCH = 4

TS = 512
LANES = 128
VMEM_LIMIT = 56 * 1024 * 1024
LOG2E = 1.4426950408889634

B_HEAD_PERM = (0, 4, 1, 5, 2, 6, 3, 7)
B_SLOPES = tuple(float(2.0 ** (-8.0 * (h + 1) / B_HEADS)) for h in range(B_HEADS))


def _dot(a, b):
    return jnp.dot(a.astype(BF16), b.astype(BF16), preferred_element_type=F32)


def _dot_nt(a, b):
    return lax.dot_general(a.astype(BF16), b.astype(BF16), (((1,), (1,)), ((), ())),
                           preferred_element_type=F32)


def _dot_tn(a, b):
    return lax.dot_general(a.astype(BF16), b.astype(BF16), (((0,), (0,)), ((), ())),
                           preferred_element_type=F32)


def _rms_rows(x, g):
    ms = jnp.mean(x * x, axis=-1, keepdims=True)
    return x * lax.rsqrt(ms + EPS) * g


def _silu(z):
    return z * jax.nn.sigmoid(z)


def _iota(shape, dim):
    return lax.broadcasted_iota(jnp.int32, shape, dim)


def _normed_input(x_ref, g_ref):
    return _rms_rows(x_ref[...], g_ref[...]).astype(BF16)


N_CHUNKS = TS // CHUNK
INTRA = 256


def _chunk_broadcast(v, row):
    v3 = v.reshape(N_CHUNKS, CHUNK, v.shape[-1])
    return jnp.broadcast_to(v3[:, row:row + 1, :], v3.shape).reshape(v.shape)


def _chunk_rows(v, row):
    return v.reshape(N_CHUNKS, CHUNK, v.shape[-1])[:, row, :]


def _chunk_block_diag(v):
    n = v.shape[0] // CHUNK
    zero = jnp.zeros((CHUNK, v.shape[-1]), v.dtype)
    return jnp.concatenate(
        [jnp.concatenate([zero] * c + [v[c * CHUNK:(c + 1) * CHUNK]] + [zero] * (n - 1 - c), axis=1)
         for c in range(n)], axis=0)


def _chunk_diag_blocks(v, width):
    n = v.shape[0] // CHUNK
    return jnp.concatenate([v[c * CHUNK:(c + 1) * CHUNK, c * width:(c + 1) * width] for c in range(n)], axis=0)


def _intra_rows(i):
    return slice(i * INTRA, (i + 1) * INTRA)


def _hgrn_kernel(x_ref, g_ref, w_ref, lb_ref, ng_ref, o_ref, st_s):
    @pl.when(pl.program_id(1) == 0)
    def _():
        st_s[...] = jnp.zeros_like(st_s)

    u = jnp.dot(_normed_input(x_ref, g_ref), w_ref[...], preferred_element_type=F32)
    q = u[:, 0:WIDTH]
    fl = u[:, WIDTH:2 * WIDTH]
    lb = lb_ref[...]
    f = lb + (1.0 - lb) * jax.nn.sigmoid(fl)
    k = (1.0 - lb) * jax.nn.sigmoid(-fl)
    vb = u[:, 2 * WIDTH:3 * WIDTH].astype(BF16)
    b = jnp.log(f)
    row_in_chunk = _iota((TS, WIDTH), 0) & (CHUNK - 1)
    s = 1
    while s < CHUNK:
        b = b + jnp.where(row_in_chunk >= s, pltpu.roll(b, s, 0), 0.0)
        s *= 2
    b_mid = _chunk_broadcast(b, CHUNK // 2 - 1)
    b_end = _chunk_broadcast(b, CHUNK - 1)
    qe = (q * jnp.exp(b - b_mid)).astype(BF16)
    ke = (k * jnp.exp(b_mid - b)).astype(BF16)
    k_end = (k * jnp.exp(b_end - b)).astype(BF16)
    qb = (q * jnp.exp(b)).astype(BF16)
    dec = jnp.exp(_chunk_rows(b, CHUNK - 1))

    ri, ci = _iota((INTRA, INTRA), 0), _iota((INTRA, INTRA), 1)
    block_causal = jnp.logical_and(ri >= ci, (ri >> 6) == (ci >> 6))
    ng = ng_ref[...]
    for h in range(A_HEADS):
        sl = slice(h * A_HEAD_DIM, (h + 1) * A_HEAD_DIM)
        intra = []
        for i in range(TS // INTRA):
            rows = slice(i * INTRA, (i + 1) * INTRA)
            sc = jnp.where(block_causal, _dot_nt(qe[rows, sl], ke[rows, sl]), 0.0)
            intra.append(_dot(sc, vb[rows, sl]))
        kv_all = jnp.concatenate([_dot_tn(vb[_intra_rows(i), sl], _chunk_block_diag(k_end[_intra_rows(i), sl]))
                                  for i in range(TS // INTRA)], axis=1)
        st = st_s[h]
        states = []
        for c in range(N_CHUNKS):
            states.append(st.astype(BF16))
            st = st * dec[c:c + 1, sl] + kv_all[:, c * A_HEAD_DIM:(c + 1) * A_HEAD_DIM]
        st_s[h] = st
        per = INTRA // CHUNK
        inter = jnp.concatenate(
            [_chunk_diag_blocks(_dot_nt(qb[_intra_rows(i), sl], jnp.concatenate(states[i * per:(i + 1) * per], axis=0)),
                                A_HEAD_DIM) for i in range(TS // INTRA)], axis=0)
        o = jnp.concatenate(intra, axis=0) + inter
        z = u[:, 3 * WIDTH + h * A_HEAD_DIM:3 * WIDTH + (h + 1) * A_HEAD_DIM]
        o_ref[:, sl] = (_rms_rows(o, ng[:, sl]) * _silu(z)).astype(o_ref.dtype)


def _ret_kernel(x_ref, g_ref, w_ref, dmat_ref, kdec_ref, qdec_ref, cdec_ref, ng_ref, o_ref, st_s):
    @pl.when(pl.program_id(1) == 0)
    def _():
        st_s[...] = jnp.zeros_like(st_s)

    u = jnp.dot(_normed_input(x_ref, g_ref), w_ref[...], preferred_element_type=F32)
    nqk = D_HEADS * D_QK
    q = u[:, 0:nqk] * (D_QK ** -0.5)
    k = u[:, nqk:2 * nqk]
    kb = k.astype(BF16)
    kd = (k * kdec_ref[...]).astype(BF16)
    vb = u[:, 2 * nqk:2 * nqk + WIDTH].astype(BF16)
    first_half = _iota((TS, LANES), 1) < D_QK
    qdec = qdec_ref[...]
    cdec = cdec_ref[...]
    ng = ng_ref[...]
    for h in range(D_HEADS):
        pair = slice((h // 2) * LANES, (h // 2 + 1) * LANES)
        vsl = slice(h * D_V, (h + 1) * D_V)
        qm = jnp.where(first_half if h % 2 == 0 else jnp.logical_not(first_half), q[:, pair], 0.0).astype(BF16)
        intra = []
        for i in range(TS // INTRA):
            rows = slice(i * INTRA, (i + 1) * INTRA)
            sc = _dot_nt(qm[rows], kb[rows, pair]) * dmat_ref[h]
            intra.append(_dot(sc, vb[rows, vsl]))
        kv_all = jnp.concatenate([_dot_tn(kd[_intra_rows(i), pair], _chunk_block_diag(vb[_intra_rows(i), vsl]))
                                  for i in range(TS // INTRA)], axis=1)
        st = st_s[h]
        states = []
        for c in range(N_CHUNKS):
            states.append(st.astype(BF16))
            st = st * cdec[:, vsl] + kv_all[:, c * D_V:(c + 1) * D_V]
        st_s[h] = st
        per = INTRA // CHUNK
        inter = jnp.concatenate(
            [_chunk_diag_blocks(_dot(qm[_intra_rows(i)], jnp.concatenate(states[i * per:(i + 1) * per], axis=1)), D_V)
             for i in range(TS // INTRA)], axis=0)
        o = jnp.concatenate(intra, axis=0) + inter * qdec[:, vsl]
        oc = o - jnp.mean(o, axis=-1, keepdims=True)
        var = jnp.mean(oc * oc, axis=-1, keepdims=True)
        z = u[:, 2 * nqk + WIDTH + h * D_V:2 * nqk + WIDTH + (h + 1) * D_V]
        o_ref[:, vsl] = (oc * lax.rsqrt(var + EPS) * ng[:, vsl] * _silu(z)).astype(o_ref.dtype)


CONV_PAD = 8
GROUP = 8


def _expand_groups(v):
    n = v.shape[0]
    sel = jnp.where((_iota((n * GROUP, n), 0) >> 3) == _iota((n * GROUP, n), 1), 1.0, 0.0).astype(BF16)
    hi = v.astype(BF16)
    r1 = v - hi.astype(F32)
    mid = r1.astype(BF16)
    lo = (r1 - mid.astype(F32)).astype(BF16)
    return (jnp.dot(sel, hi, preferred_element_type=F32) + jnp.dot(sel, mid, preferred_element_type=F32)
            + jnp.dot(sel, lo, preferred_element_type=F32))


def _scan_steps(a, b, row, length, axis=0):
    s = 1
    while s < length:
        keep = row >= s
        a_prev = jnp.where(keep, pltpu.roll(a, s, axis), 1.0)
        b_prev = jnp.where(keep, pltpu.roll(b, s, axis), 0.0)
        b = a * b_prev + b
        a = a * a_prev
        s *= 2
    return a, b


def _rglru_kernel(x_ref, g_ref, w_ref, cw_ref, cb_ref, wra_ref, bra_ref, wri_ref, bri_ref, lam_ref,
                  o_ref, ext_s, h_s, a_s, b_s):
    t = pl.program_id(1)

    @pl.when(t == 0)
    def _():
        ext_s[0:CONV_PAD, :] = jnp.zeros((CONV_PAD, WIDTH), F32)
        h_s[...] = jnp.zeros_like(h_s)

    @pl.when(t > 0)
    def _():
        ext_s[0:CONV_PAD, :] = ext_s[TS:TS + CONV_PAD, :]

    u = jnp.dot(_normed_input(x_ref, g_ref), w_ref[...], preferred_element_type=F32)
    ext_s[CONV_PAD:CONV_PAD + TS, :] = u[:, 0:WIDTH]
    cz = u[:, WIDTH:2 * WIDTH]
    cw = cw_ref[...]
    xc = cb_ref[...] + cw[CONV_WIDTH - 1:CONV_WIDTH, :] * u[:, 0:WIDTH]
    for j in range(CONV_WIDTH - 1):
        back = CONV_WIDTH - 1 - j
        xc = xc + cw[j:j + 1, :] * ext_s[CONV_PAD - back:CONV_PAD - back + TS, :]
    r = jax.nn.sigmoid(_dot(xc, wra_ref[...]) + bra_ref[...])
    ig = jax.nn.sigmoid(_dot(xc, wri_ref[...]) + bri_ref[...])
    neg_lam = -lam_ref[...]
    softplus = jnp.maximum(neg_lam, 0.0) + jnp.log1p(jnp.exp(-jnp.abs(neg_lam)))
    log_a = -RG_C * r * softplus
    a = jnp.exp(log_a)
    th = jnp.tanh(log_a)
    b = jnp.sqrt(-2.0 * th / (1.0 - th)) * (ig * xc)
    n_groups = TS // GROUP
    grouped = (n_groups, GROUP, WIDTH)
    a, b = _scan_steps(a.reshape(grouped), b.reshape(grouped), _iota(grouped, 1), GROUP, axis=1)
    a, b = a.reshape(TS, WIDTH), b.reshape(TS, WIDTH)
    last_rows = pl.ds(GROUP - 1, n_groups, stride=GROUP)
    for j in range(WIDTH // LANES):
        a_s[j] = a[:, j * LANES:(j + 1) * LANES]
        b_s[j] = b[:, j * LANES:(j + 1) * LANES]
    a_last = jnp.concatenate([a_s[j, last_rows, :] for j in range(WIDTH // LANES)], axis=1)
    b_last = jnp.concatenate([b_s[j, last_rows, :] for j in range(WIDTH // LANES)], axis=1)
    group = _iota((n_groups, WIDTH), 0)
    a_last, b_last = _scan_steps(a_last, b_last, group, n_groups)
    h0 = h_s[0:1, :]
    h_after = a_last * h0 + b_last
    h_before = jnp.where(group == 0, h0, pltpu.roll(h_after, 1, 0))
    h = a * _expand_groups(h_before) + b
    h_s[0:1, :] = h_after[n_groups - 1:n_groups, :]
    o_ref[...] = (h * _silu(cz)).astype(o_ref.dtype)


NSA_Q = 0
NSA_KC = 512
NSA_VC = 640
NSA_KS = 768
NSA_VS = 896
NSA_KW = 1024
NSA_VW = 1152
NSA_G = 1280
NSA_Z = 1408
NSA_COLS = 1920
SEL_KEYS = TS
WIN_Q = 256
WIN_KEYS = WIN_Q + WINDOW
CMP_PHASES = CMP_STRIDE
AUG = 2 * LANES
POS_ROWS = 16
MASK_BIG = float(2.0 ** 40)
SUM_ROWS = 16
SEL_COLS = 512
SEL_AHEAD = 2


def _half_rms(v, first_half):
    v2 = v * v
    s0 = jnp.sum(jnp.where(first_half, v2, 0.0), axis=-1, keepdims=True)
    s1 = jnp.sum(jnp.where(first_half, 0.0, v2), axis=-1, keepdims=True)
    ms = jnp.where(first_half, s0, s1) * (1.0 / B_HEAD_DIM)
    return v * lax.rsqrt(ms + EPS)


def _softmax_cols(s):
    p = jnp.exp2(s - jnp.max(s, axis=0, keepdims=True))
    return p, jnp.sum(p, axis=0, keepdims=True)


def _nsa_kernel(seq, x_ref, g_ref, w_ref, qg_ref, kg_ref, pos12_ref, w12k_ref, w12v_ref,
                ov_ref, slope_ref, possel_ref, poswin_ref, poscmp_ref, o_ref,
                u_s, ks_c, vst_c, kw_c, vwt_c, kc_c, vc_c, p1k_s, p1v_s, kv_s):
    n_cmp = seq // CMP_STRIDE
    n_sel = seq // SEL_LEN
    ti = pl.program_id(1)

    @pl.when(ti == 0)
    def _():
        for ref in (ks_c, vst_c, kw_c, vwt_c, kc_c, vc_c, p1k_s, p1v_s):
            ref[...] = jnp.zeros_like(ref)

    u_s[...] = jnp.dot(_normed_input(x_ref, g_ref), w_ref[...], preferred_element_type=F32)
    row0 = pl.multiple_of(ti * TS, TS)
    kg = kg_ref[...]
    half_t = _iota((TS, LANES), 1) < B_HEAD_DIM

    ks_c[pl.ds(row0, TS), :] = (_half_rms(u_s[:, NSA_KS:NSA_KS + B_KV], half_t) * kg).astype(BF16)
    vst_c[ti] = u_s[:, NSA_VS:NSA_VS + B_KV].T.astype(BF16)
    kw_c[pl.ds(row0 + WINDOW, TS), :] = (_half_rms(u_s[:, NSA_KW:NSA_KW + B_KV], half_t) * kg).astype(BF16)
    for i in range(TS // Q_BLOCK):
        vwt_c[ti * (TS // Q_BLOCK) + WINDOW // Q_BLOCK + i] = (
            u_s[i * Q_BLOCK:(i + 1) * Q_BLOCK, NSA_VW:NSA_VW + B_KV].T.astype(BF16))

    n_grp = TS // CMP_STRIDE
    first_row = _iota((n_grp, B_KV), 0) == 0
    half_g = _iota((n_grp, LANES), 1) < B_HEAD_DIM
    crow = pl.ds(pl.multiple_of(ti * n_grp, n_grp), n_grp)
    for i, (col, w12_ref, p1_s, cache, is_key) in enumerate(((NSA_KC, w12k_ref, p1k_s, kc_c, True),
                                                             (NSA_VC, w12v_ref, p1v_s, vc_c, False))):
        kv_s[i] = u_s[:, col:col + B_KV]
        grouped = jnp.concatenate([kv_s[i, pl.ds(r, n_grp, stride=CMP_STRIDE), :] for r in range(CMP_PHASES)],
                                  axis=1)
        both = _dot(jnp.concatenate([grouped, pos12_ref[...]], axis=0), w12_ref[...])
        p1 = both[0:n_grp, 0:B_KV] + both[n_grp:n_grp + 1, 0:B_KV]
        p2 = both[0:n_grp, B_KV:2 * B_KV] + both[n_grp + 1:n_grp + 2, B_KV:2 * B_KV]
        p1_prev = jnp.where(first_row, p1_s[n_grp - 1:n_grp, :], pltpu.roll(p1, 1, 0))
        p1_s[...] = p1
        blk = p1_prev + p2
        if is_key:
            blk = _half_rms(blk, half_g) * kg
        cache[crow, :] = blk.astype(BF16)

    qg = qg_ref[...]
    n_q = B_REP * TS
    halves = TS // WIN_Q

    def cols(a):
        return jnp.concatenate([a[:, hb * WIN_Q:(hb + 1) * WIN_Q] for hb in range(halves) for _ in range(B_REP)],
                               axis=1)

    def col_block(hb, r):
        return slice((hb * B_REP + r) * WIN_Q, (hb * B_REP + r + 1) * WIN_Q)

    q_n = [_half_rms(u_s[:, NSA_Q + r * LANES:NSA_Q + (r + 1) * LANES], half_t)
           * qg[:, r * LANES:(r + 1) * LANES] * (B_HEAD_DIM ** -0.5 * LOG2E) for r in range(B_REP)]
    q_t = jnp.concatenate([q_n[r][hb * WIN_Q:(hb + 1) * WIN_Q].T for hb in range(halves) for r in range(B_REP)],
                          axis=1)
    q_rows_first = _iota((LANES, n_q), 0) < B_HEAD_DIM
    base = [jnp.concatenate([jnp.where(q_rows_first, q_t, 0.0).astype(BF16), slope_ref[0]], axis=0),
            jnp.concatenate([jnp.where(q_rows_first, 0.0, q_t).astype(BF16), slope_ref[1]], axis=0)]
    n_base = LANES + POS_ROWS
    groups = range(B_GROUPS)

    t_c = row0 + _iota((n_cmp, TS), 1)
    nprime = _iota((n_cmp, TS), 0)
    madd_c = cols(jnp.where(t_c >= nprime * CMP_STRIDE + (CMP_STRIDE - 1),
                            jnp.where(nprime >= 1, 0.0, NEG_INF), NEG_INF))
    any_c = cols(jnp.where(row0 + _iota((1, TS), 1) >= CMP_LEN - 1, 1.0, 0.0))
    kaug_c = jnp.concatenate([kc_c[...], poscmp_ref[...]], axis=1)
    vct = vc_c[...].astype(F32).T.astype(BF16)
    zeros_c = jnp.zeros((AUG - n_base, n_q), BF16)
    s_c = [jnp.dot(kaug_c, jnp.concatenate([base[g], zeros_c], axis=0), preferred_element_type=F32) + madd_c
           for g in groups]
    p_c = []
    for g in groups:
        p, l = _softmax_cols(s_c[g])
        p_c.append(p * (any_c / l))
    o_cmp = [_dot(vct[g * B_HEAD_DIM:(g + 1) * B_HEAD_DIM], p_c[g]) for g in groups]
    blk_t = (row0 + _iota((n_sel, TS), 1)) >> 6
    jrow = _iota((n_sel, TS), 0)
    q_aug = []
    for g in groups:
        p_sum = jnp.concatenate(
            [sum(p_c[g][:, col_block(hb, r)] for r in range(B_REP)) for hb in range(halves)], axis=1)
        imp = jnp.dot(ov_ref[...], p_sum, precision=lax.Precision.HIGHEST,
                      preferred_element_type=F32)
        imp = jnp.where(jrow == blk_t, BIG, jnp.where(jrow < blk_t, imp, -BIG))
        cnt = []
        for v0 in range(0, n_sel, 8):
            mine, jmine = imp[v0:v0 + 8], v0 + _iota((8, TS), 0)
            c = jnp.zeros((8, TS), F32)
            for jp in range(n_sel):
                other = imp[jp:jp + 1, :]
                if jp < v0:
                    ahead = other >= mine
                elif jp >= v0 + 8:
                    ahead = other > mine
                else:
                    ahead = jnp.where(jmine > jp, jnp.where(other >= mine, 1.0, 0.0),
                                      jnp.where(other > mine, 1.0, 0.0)) > 0.5
                c = c + jnp.where(ahead, 1.0, 0.0)
            cnt.append(c)
        sel_neg = jnp.where(jnp.concatenate(cnt, axis=0) < SEL_TOPN, 0.0, -MASK_BIG).astype(BF16)
        q_aug.append(jnp.concatenate([base[g], cols(sel_neg),
                                      jnp.zeros((AUG - n_base - n_sel, n_q), BF16)], axis=0))

    def sel_step(kc, st, diagonal):
        k0 = pl.multiple_of(kc * SEL_KEYS, SEL_KEYS)
        kaug = jnp.concatenate([ks_c[pl.ds(k0, SEL_KEYS), :], possel_ref[pl.ds(k0, SEL_KEYS), :]], axis=1)
        ones = jnp.ones((SUM_ROWS, SEL_KEYS), BF16)
        v_aug = [jnp.concatenate([vst_c[kc, g * B_HEAD_DIM:(g + 1) * B_HEAD_DIM, :], ones], axis=0) for g in groups]
        if diagonal:
            madd = cols(jnp.where(_iota((SEL_KEYS, TS), 0) <= _iota((SEL_KEYS, TS), 1), 0.0, NEG_INF))
        units = [(g, slice(b * SEL_COLS, (b + 1) * SEL_COLS)) for b in range(n_q // SEL_COLS) for g in groups]

        def qk(g, cs):
            s = jnp.dot(kaug, q_aug[g][:, cs], preferred_element_type=F32)
            return s + madd[:, cs] if diagonal else s

        scores = [qk(*u) for u in units[:SEL_AHEAD]]
        m_out, acc_out = [[] for _ in groups], [[] for _ in groups]
        for i, (g, cs) in enumerate(units):
            if i + SEL_AHEAD < len(units):
                scores.append(qk(*units[i + SEL_AHEAD]))
            s = scores[i]
            m_old = st[2 * g][:, cs]
            m_new = jnp.maximum(m_old, jnp.max(s, axis=0, keepdims=True))
            p = jnp.exp2((s - m_new).astype(BF16))
            m_out[g].append(m_new)
            acc_out[g].append(jnp.exp2(m_old - m_new) * st[2 * g + 1][:, cs]
                              + jnp.dot(v_aug[g], p, preferred_element_type=F32))
        out = []
        for g in groups:
            out += [jnp.concatenate(m_out[g], axis=1), jnp.concatenate(acc_out[g], axis=1)]
        return tuple(out)

    init = []
    for g in groups:
        init += [jnp.full((1, n_q), NEG_INF, F32), jnp.zeros((B_HEAD_DIM + SUM_ROWS, n_q), F32)]
    st = lax.fori_loop(0, ti, functools.partial(sel_step, diagonal=False), tuple(init))
    st = sel_step(ti, st, diagonal=True)
    o_sel = [st[2 * g + 1][0:B_HEAD_DIM] / st[2 * g + 1][B_HEAD_DIM:B_HEAD_DIM + 1] for g in groups]

    o_win_halves = []
    for hb in range(halves):
        w0 = row0 + hb * WIN_Q
        kaug_w = jnp.concatenate([kw_c[pl.ds(w0, WIN_KEYS), :], poswin_ref[pl.ds(w0, WIN_KEYS), :]], axis=1)
        wblk = ti * (TS // Q_BLOCK) + hb * (WIN_Q // Q_BLOCK)
        ones = jnp.ones((SUM_ROWS, WIN_KEYS), BF16)
        vwt = [jnp.concatenate(
            [jnp.concatenate([vwt_c[wblk + i, g * B_HEAD_DIM:(g + 1) * B_HEAD_DIM, :]
                              for i in range(WIN_KEYS // Q_BLOCK)], axis=1), ones], axis=0) for g in groups]
        spos = (w0 - WINDOW) + _iota((WIN_KEYS, WIN_Q), 0)
        dist_w = (w0 + _iota((WIN_KEYS, WIN_Q), 1)) - spos
        madd_w = jnp.where(dist_w >= 0, jnp.where(dist_w < WINDOW, jnp.where(spos >= 0, 0.0, NEG_INF), NEG_INF),
                           NEG_INF)
        madd_w = jnp.concatenate([madd_w] * B_REP, axis=1)
        qs = slice(hb * B_REP * WIN_Q, (hb + 1) * B_REP * WIN_Q)
        s_w = [jnp.dot(kaug_w, q_aug[g][:, qs], preferred_element_type=F32) + madd_w for g in groups]
        p_w = [jnp.exp2((s - jnp.max(s, axis=0, keepdims=True)).astype(BF16)) for s in s_w]
        pv = [jnp.dot(vwt[g], p_w[g], preferred_element_type=F32) for g in groups]
        o_win_halves.append([x[0:B_HEAD_DIM] / x[B_HEAD_DIM:B_HEAD_DIM + 1] for x in pv])
    o_win = [jnp.concatenate([o_win_halves[hb][g] for hb in range(halves)], axis=1) for g in groups]

    gates_t = jax.nn.sigmoid(u_s[:, NSA_G:NSA_G + LANES]).T
    for hb in range(halves):
        ts = slice(hb * WIN_Q, (hb + 1) * WIN_Q)
        for r in range(B_REP):
            cs = col_block(hb, r)
            acc = jnp.zeros((LANES, WIN_Q), F32)
            for c, branch in enumerate((o_cmp, o_sel, o_win)):
                acc = acc + jnp.concatenate(
                    [branch[0][:, cs] * gates_t[r * 3 + c:r * 3 + c + 1, ts],
                     branch[1][:, cs] * gates_t[(B_REP + r) * 3 + c:(B_REP + r) * 3 + c + 1, ts]], axis=0)
            z = u_s[ts, NSA_Z + r * LANES:NSA_Z + (r + 1) * LANES]
            o_ref[ts, r * LANES:(r + 1) * LANES] = (acc.T * _silu(z)).astype(o_ref.dtype)


def _merge_kernel(x_ref, oa_ref, ob_ref, oc_ref, od_ref, g_ref, wmg_ref, mb_ref, wbr_ref, wout_ref, o_ref):
    x = x_ref[...]
    xn = _rms_rows(x, g_ref[...]).astype(BF16)
    merged = jnp.zeros((TS, D_MODEL), F32)
    for br, o_k in enumerate((oa_ref, ob_ref, oc_ref, od_ref)):
        cols = slice(br * D_MODEL, (br + 1) * D_MODEL)
        gate = jax.nn.sigmoid(jnp.dot(xn, wmg_ref[:, cols], preferred_element_type=F32) + mb_ref[:, cols])
        merged = merged + gate * jnp.dot(o_k[...], wbr_ref[br], preferred_element_type=F32)
    o_ref[...] = x + jnp.dot(merged.astype(BF16), wout_ref[...], preferred_element_type=F32)


def _const_spec(shape):
    nd = len(shape)
    return pl.BlockSpec(shape, lambda b, t: (0,) * nd, pipeline_mode=pl.Buffered(1))


def _tile_spec(nb, width):
    return pl.BlockSpec((nb, TS, width), lambda b, t: (b, t, 0))


def _per_batch(body, n_tiles, n_consts, nb):
    def kernel(*refs):
        tiles, consts = refs[:n_tiles], refs[n_tiles:n_tiles + n_consts]
        out, scratch = refs[n_tiles + n_consts], refs[n_tiles + n_consts + 1:]
        for i in range(nb):
            body(*[t.at[i] for t in tiles], *consts, out.at[i], *[s.at[i] for s in scratch])
    return kernel


def _call(body, name, x, consts, out_width, out_dtype, scratch, extra_tiles=(), nb=1):
    batch, seq, _ = x.shape
    assert batch % nb == 0
    in_specs = [_tile_spec(nb, D_MODEL)] + [_tile_spec(nb, a.shape[-1]) for a in extra_tiles]
    in_specs += [_const_spec(c.shape) for c in consts]
    scratch = [pltpu.VMEM((nb,) + tuple(s.shape), s.dtype) for s in scratch]
    return pl.pallas_call(
        _per_batch(body, 1 + len(extra_tiles), len(consts), nb),
        name=name,
        grid=(batch // nb, seq // TS),
        in_specs=in_specs,
        out_specs=_tile_spec(nb, out_width),
        out_shape=jax.ShapeDtypeStruct((batch, seq, out_width), out_dtype),
        scratch_shapes=scratch,
        compiler_params=pltpu.CompilerParams(
            dimension_semantics=("arbitrary", "arbitrary"), vmem_limit_bytes=VMEM_LIMIT),
    )(x, *extra_tiles, *consts)


def _retention_tables():
    pos = np.arange(TS, dtype=np.float64)
    log_g = np.log1p(-np.exp2(-5.0 - np.arange(D_HEADS, dtype=np.float64)))
    rel = pos[:, None] - pos[None, :]
    same = (pos[:, None] // CHUNK) == (pos[None, :] // CHUNK)
    dmat = np.where((rel >= 0) & same, np.exp(log_g[:, None, None] * np.maximum(rel, 0.0)), 0.0)
    dmat = dmat[:, :INTRA, :INTRA]
    inpos = pos % CHUNK
    kdec = np.repeat(np.exp(log_g[None, :] * (CHUNK - 1.0 - inpos)[:, None]), D_QK, axis=1)
    qdec = np.repeat(np.exp(log_g[None, :] * (inpos + 1.0)[:, None]), D_V, axis=1)
    cdec = np.repeat(np.exp(log_g * CHUNK)[None, :], D_V, axis=1)
    return tuple(jnp.asarray(a, F32) for a in (dmat, kdec, qdec, cdec))


def _bf16_pieces(v, n):
    out = []
    rem = np.asarray(v, np.float64)
    for _ in range(n):
        piece = rem.astype(BF16).astype(np.float64)
        out.append(piece)
        rem = rem - piece
    return out


def _position_tile(pos, block_ids=None):
    pos = np.maximum(np.asarray(pos), 0)
    tab = np.zeros((pos.shape[0], LANES), np.float32)
    for i in range(3):
        tab[:, 2 * i] = (pos // 256) * 256
        tab[:, 2 * i + 1] = pos % 256
    if block_ids is not None:
        tab[np.arange(pos.shape[0]), POS_ROWS + np.asarray(block_ids)] = 1.0
    return jnp.asarray(tab, BF16)


def _nsa_tables(seq):
    n_cmp = seq // CMP_STRIDE
    n_sel = seq // SEL_LEN
    n = np.arange(n_cmp) - 1
    cmp_start = n * CMP_STRIDE
    sel_start = np.arange(n_sel) * SEL_LEN
    ov = ((cmp_start[None, :] < sel_start[:, None] + SEL_LEN) & (cmp_start[None, :] + CMP_LEN > sel_start[:, None])
          & (n[None, :] >= 0) & (n[None, :] <= (seq - CMP_LEN) // CMP_STRIDE))
    slope = np.zeros((B_GROUPS, POS_ROWS, TS // WIN_Q, B_REP, WIN_Q), np.float32)
    for g in range(B_GROUPS):
        for r in range(B_REP):
            pieces = _bf16_pieces(B_SLOPES[g * B_REP + r] * LOG2E, 3)
            for i, piece in enumerate(pieces):
                slope[g, 2 * i:2 * i + 2, :, r, :] = piece
    slope = slope.reshape(B_GROUPS, POS_ROWS, B_REP * TS)
    tpos = np.arange(seq)
    possel = _position_tile(tpos, tpos // SEL_LEN)
    poswin = _position_tile(np.arange(seq + WINDOW) - WINDOW)
    poscmp = _position_tile(np.arange(n_cmp) * CMP_STRIDE + (CMP_STRIDE - 1))
    return (jnp.asarray(ov, F32), jnp.asarray(slope, BF16), possel, poswin, poscmp)


def _perm_heads(w):
    lead = w.shape[:-1]
    return w.reshape(lead + (B_HEADS, B_HEAD_DIM))[..., B_HEAD_PERM, :].reshape(lead + (WIDTH,))


def _cmp_weights(w):
    w = w.reshape(2, CMP_PHASES, B_HEAD_DIM, B_HEAD_DIM)
    eye = jnp.eye(B_GROUPS, dtype=w.dtype)
    big = jnp.einsum('spde,gh->spgdhe', w, eye).reshape(2, CMP_PHASES * B_KV, B_KV)
    return jnp.concatenate([big[0], big[1]], axis=1).astype(BF16)


def _cmp_pos(pos):
    p = jnp.tile(pos.reshape(2, CMP_PHASES, 1, B_HEAD_DIM), (1, 1, B_GROUPS, 1)).reshape(2, CMP_PHASES * B_KV)
    return jnp.concatenate([p, jnp.zeros((14, CMP_PHASES * B_KV), p.dtype)], axis=0)


def _block_diag(w):
    eye = jnp.eye(C_BLOCKS, dtype=w.dtype)
    return jnp.einsum('ncd,nm->ncmd', w, eye).reshape(WIDTH, WIDTH).astype(BF16)


def kernel(x, norm_g, w_in, lb_logits, a_norm_g, b_q_norm_g, b_k_norm_g, b_cmp_pos, b_cmp_wk, b_cmp_wv,
           c_conv_w, c_conv_b, c_w_ra, c_b_ra, c_w_ri, c_b_ri, c_lambda, d_norm_g, merge_b, w_branch, w_out):
    batch, seq, _ = x.shape
    depth = norm_g.shape[0]
    assert seq % TS == 0 and x.shape[-1] == D_MODEL

    p_lb = jax.nn.softmax(lb_logits.astype(F32), axis=0)
    lower_bounds = jnp.cumsum(p_lb, axis=0) - p_lb[0:1]
    dmat, kdec, qdec, cdec = _retention_tables()
    nsa_tabs = _nsa_tables(seq)
    row = lambda v: v.reshape(1, -1).astype(F32)
    vm = lambda shape, dt=F32: pltpu.VMEM(shape, dt)
    n_cmp = seq // CMP_STRIDE

    for l in range(depth):
        w = w_in[l]
        g = row(norm_g[l])
        w_a = w[:, 0:2048].astype(BF16)
        b0 = 2048
        gate_cols = jnp.pad(w[:, b0 + 1280:b0 + 1304], ((0, 0), (0, LANES - 3 * B_HEADS)))
        w_b = jnp.concatenate([_perm_heads(w[:, b0:b0 + 512]), w[:, b0 + 512:b0 + 1280], gate_cols,
                               _perm_heads(w[:, b0 + 1304:b0 + 1816])], axis=1).astype(BF16)
        c0 = b0 + 1816
        w_c = w[:, c0:c0 + 1024].astype(BF16)
        d0 = c0 + 1024
        w_d = w[:, d0:d0 + 1536].astype(BF16)
        m0 = d0 + 1536
        w_m = w[:, m0:m0 + N_BRANCH * D_MODEL].astype(BF16)

        o_a = _call(_hgrn_kernel, "hgrn", x, [g, w_a, row(lower_bounds[l]), row(a_norm_g[l])], WIDTH, BF16,
                    [vm((A_HEADS, A_HEAD_DIM, A_HEAD_DIM))])

        o_b = _call(functools.partial(_nsa_kernel, seq), "nsa", x,
                    [g, w_b, row(jnp.tile(b_q_norm_g[l], B_HEADS)), row(jnp.tile(b_k_norm_g[l], B_GROUPS)),
                     _cmp_pos(b_cmp_pos[l]), _cmp_weights(b_cmp_wk[l]), _cmp_weights(b_cmp_wv[l]), *nsa_tabs],
                    WIDTH, BF16,
                    [vm((TS, NSA_COLS)), vm((seq, B_KV), BF16), vm((seq // TS, B_KV, TS), BF16),
                     vm((seq + WINDOW, B_KV), BF16), vm(((seq + WINDOW) // Q_BLOCK, B_KV, Q_BLOCK), BF16),
                     vm((n_cmp, B_KV), BF16), vm((n_cmp, B_KV), BF16),
                     vm((TS // CMP_STRIDE, B_KV)), vm((TS // CMP_STRIDE, B_KV)), vm((2, TS, B_KV))])

        o_c = _call(_rglru_kernel, "rglru", x,
                    [g, w_c, c_conv_w[l].astype(F32), row(c_conv_b[l]), _block_diag(c_w_ra[l]), row(c_b_ra[l]),
                     _block_diag(c_w_ri[l]), row(c_b_ri[l]), row(c_lambda[l])], WIDTH, BF16,
                    [vm((TS + 2 * CONV_PAD, WIDTH)), vm((8, WIDTH)), vm((WIDTH // LANES, TS, LANES)),
                     vm((WIDTH // LANES, TS, LANES))])

        o_d = _call(_ret_kernel, "retention", x, [g, w_d, dmat, kdec, qdec, cdec, row(d_norm_g[l])], WIDTH, BF16,
                    [vm((D_HEADS, LANES, D_V))])

        wbr = w_branch[l]
        wbr = jnp.stack([wbr[0], _perm_heads(wbr[1].T).T, wbr[2], wbr[3]], axis=0).astype(BF16)
        x = _call(_merge_kernel, "merge", x, [g, w_m, row(merge_b[l]), wbr, w_out[l].astype(BF16)],
                  D_MODEL, x.dtype, [], extra_tiles=(o_a, o_b, o_c, o_d))
    return x
```

```python
import functools

import numpy as np
import jax
import jax.numpy as jnp
from jax import lax
from jax.experimental import pallas as pl
from jax.experimental.pallas import tpu as pltpu

F32 = jnp.float32
BF16 = jnp.bfloat16

D_MODEL = 1024
WIDTH = 512
EPS = 1e-6
NEG_INF = -1e30
BIG = 1e30

A_HEADS = 4
A_HEAD_DIM = 128
CHUNK = 64

B_HEADS = 8
B_GROUPS = 2
B_REP = B_HEADS // B_GROUPS
B_HEAD_DIM = 64
B_KV = B_GROUPS * B_HEAD_DIM
CMP_LEN = 32
CMP_STRIDE = 16
SEL_LEN = 64
SEL_TOPN = 8
WINDOW = 256
Q_BLOCK = 128

C_BLOCKS = 8
C_BLOCK_DIM = 64
CONV_WIDTH = 4
RG_C = 8.0

D_HEADS = 4
D_QK = 64
D_V = 128

N_BRANCH = 4

TS = 512
LANES = 128
VMEM_LIMIT = 56 * 1024 * 1024
LOG2E = 1.4426950408889634

B_HEAD_PERM = (0, 4, 1, 5, 2, 6, 3, 7)
B_SLOPES = tuple(float(2.0 ** (-8.0 * (h + 1) / B_HEADS)) for h in range(B_HEADS))


def _dot(a, b):
    return jnp.dot(a.astype(BF16), b.astype(BF16), preferred_element_type=F32)


def _dot_nt(a, b):
    return lax.dot_general(a.astype(BF16), b.astype(BF16), (((1,), (1,)), ((), ())),
                           preferred_element_type=F32)


def _dot_tn(a, b):
    return lax.dot_general(a.astype(BF16), b.astype(BF16), (((0,), (0,)), ((), ())),
                           preferred_element_type=F32)


def _rms_rows(x, g):
    ms = jnp.mean(x * x, axis=-1, keepdims=True)
    return x * lax.rsqrt(ms + EPS) * g


def _silu(z):
    return z * jax.nn.sigmoid(z)


def _iota(shape, dim):
    return lax.broadcasted_iota(jnp.int32, shape, dim)


def _normed_input(x_ref, g_ref):
    return _rms_rows(x_ref[...], g_ref[...]).astype(BF16)


N_CHUNKS = TS // CHUNK
INTRA = 256


def _chunk_broadcast(v, row):
    v3 = v.reshape(N_CHUNKS, CHUNK, v.shape[-1])
    return jnp.broadcast_to(v3[:, row:row + 1, :], v3.shape).reshape(v.shape)


def _chunk_rows(v, row):
    return v.reshape(N_CHUNKS, CHUNK, v.shape[-1])[:, row, :]


def _chunk_block_diag(v):
    n = v.shape[0] // CHUNK
    zero = jnp.zeros((CHUNK, v.shape[-1]), v.dtype)
    return jnp.concatenate(
        [jnp.concatenate([zero] * c + [v[c * CHUNK:(c + 1) * CHUNK]] + [zero] * (n - 1 - c), axis=1)
         for c in range(n)], axis=0)


def _chunk_diag_blocks(v, width):
    n = v.shape[0] // CHUNK
    return jnp.concatenate([v[c * CHUNK:(c + 1) * CHUNK, c * width:(c + 1) * width] for c in range(n)], axis=0)


def _intra_rows(i):
    return slice(i * INTRA, (i + 1) * INTRA)


def _hgrn_kernel(x_ref, g_ref, w_ref, lb_ref, ng_ref, o_ref, st_s):
    @pl.when(pl.program_id(1) == 0)
    def _():
        st_s[...] = jnp.zeros_like(st_s)

    u = jnp.dot(_normed_input(x_ref, g_ref), w_ref[...], preferred_element_type=F32)
    q = u[:, 0:WIDTH]
    fl = u[:, WIDTH:2 * WIDTH]
    lb = lb_ref[...]
    f = lb + (1.0 - lb) * jax.nn.sigmoid(fl)
    k = (1.0 - lb) * jax.nn.sigmoid(-fl)
    vb = u[:, 2 * WIDTH:3 * WIDTH].astype(BF16)
    b = jnp.log(f)
    row_in_chunk = _iota((TS, WIDTH), 0) & (CHUNK - 1)
    s = 1
    while s < CHUNK:
        b = b + jnp.where(row_in_chunk >= s, pltpu.roll(b, s, 0), 0.0)
        s *= 2
    b_mid = _chunk_broadcast(b, CHUNK // 2 - 1)
    b_end = _chunk_broadcast(b, CHUNK - 1)
    qe = (q * jnp.exp(b - b_mid)).astype(BF16)
    ke = (k * jnp.exp(b_mid - b)).astype(BF16)
    k_end = (k * jnp.exp(b_end - b)).astype(BF16)
    qb = (q * jnp.exp(b)).astype(BF16)
    dec = jnp.exp(_chunk_rows(b, CHUNK - 1))

    ri, ci = _iota((INTRA, INTRA), 0), _iota((INTRA, INTRA), 1)
    block_causal = jnp.logical_and(ri >= ci, (ri >> 6) == (ci >> 6))
    ng = ng_ref[...]
    for h in range(A_HEADS):
        sl = slice(h * A_HEAD_DIM, (h + 1) * A_HEAD_DIM)
        intra = []
        for i in range(TS // INTRA):
            rows = slice(i * INTRA, (i + 1) * INTRA)
            sc = jnp.where(block_causal, _dot_nt(qe[rows, sl], ke[rows, sl]), 0.0)
            intra.append(_dot(sc, vb[rows, sl]))
        kv_all = jnp.concatenate([_dot_tn(vb[_intra_rows(i), sl], _chunk_block_diag(k_end[_intra_rows(i), sl]))
                                  for i in range(TS // INTRA)], axis=1)
        st = st_s[h]
        states = []
        for c in range(N_CHUNKS):
            states.append(st.astype(BF16))
            st = st * dec[c:c + 1, sl] + kv_all[:, c * A_HEAD_DIM:(c + 1) * A_HEAD_DIM]
        st_s[h] = st
        per = INTRA // CHUNK
        inter = jnp.concatenate(
            [_chunk_diag_blocks(_dot_nt(qb[_intra_rows(i), sl], jnp.concatenate(states[i * per:(i + 1) * per], axis=0)),
                                A_HEAD_DIM) for i in range(TS // INTRA)], axis=0)
        o = jnp.concatenate(intra, axis=0) + inter
        z = u[:, 3 * WIDTH + h * A_HEAD_DIM:3 * WIDTH + (h + 1) * A_HEAD_DIM]
        o_ref[:, sl] = (_rms_rows(o, ng[:, sl]) * _silu(z)).astype(o_ref.dtype)


def _ret_kernel(x_ref, g_ref, w_ref, dmat_ref, kdec_ref, qdec_ref, cdec_ref, ng_ref, o_ref, st_s):
    @pl.when(pl.program_id(1) == 0)
    def _():
        st_s[...] = jnp.zeros_like(st_s)

    u = jnp.dot(_normed_input(x_ref, g_ref), w_ref[...], preferred_element_type=F32)
    nqk = D_HEADS * D_QK
    q = u[:, 0:nqk] * (D_QK ** -0.5)
    k = u[:, nqk:2 * nqk]
    kb = k.astype(BF16)
    kd = (k * kdec_ref[...]).astype(BF16)
    vb = u[:, 2 * nqk:2 * nqk + WIDTH].astype(BF16)
    first_half = _iota((TS, LANES), 1) < D_QK
    qdec = qdec_ref[...]
    cdec = cdec_ref[...]
    ng = ng_ref[...]
    for h in range(D_HEADS):
        pair = slice((h // 2) * LANES, (h // 2 + 1) * LANES)
        vsl = slice(h * D_V, (h + 1) * D_V)
        qm = jnp.where(first_half if h % 2 == 0 else jnp.logical_not(first_half), q[:, pair], 0.0).astype(BF16)
        intra = []
        for i in range(TS // INTRA):
            rows = slice(i * INTRA, (i + 1) * INTRA)
            sc = _dot_nt(qm[rows], kb[rows, pair]) * dmat_ref[h]
            intra.append(_dot(sc, vb[rows, vsl]))
        kv_all = jnp.concatenate([_dot_tn(kd[_intra_rows(i), pair], _chunk_block_diag(vb[_intra_rows(i), vsl]))
                                  for i in range(TS // INTRA)], axis=1)
        st = st_s[h]
        states = []
        for c in range(N_CHUNKS):
            states.append(st.astype(BF16))
            st = st * cdec[:, vsl] + kv_all[:, c * D_V:(c + 1) * D_V]
        st_s[h] = st
        per = INTRA // CHUNK
        inter = jnp.concatenate(
            [_chunk_diag_blocks(_dot(qm[_intra_rows(i)], jnp.concatenate(states[i * per:(i + 1) * per], axis=1)), D_V)
             for i in range(TS // INTRA)], axis=0)
        o = jnp.concatenate(intra, axis=0) + inter * qdec[:, vsl]
        oc = o - jnp.mean(o, axis=-1, keepdims=True)
        var = jnp.mean(oc * oc, axis=-1, keepdims=True)
        z = u[:, 2 * nqk + WIDTH + h * D_V:2 * nqk + WIDTH + (h + 1) * D_V]
        o_ref[:, vsl] = (oc * lax.rsqrt(var + EPS) * ng[:, vsl] * _silu(z)).astype(o_ref.dtype)


CONV_PAD = 8
GROUP = 8


def _expand_groups(v):
    n = v.shape[0]
    sel = jnp.where((_iota((n * GROUP, n), 0) >> 3) == _iota((n * GROUP, n), 1), 1.0, 0.0).astype(BF16)
    hi = v.astype(BF16)
    r1 = v - hi.astype(F32)
    mid = r1.astype(BF16)
    lo = (r1 - mid.astype(F32)).astype(BF16)
    return (jnp.dot(sel, hi, preferred_element_type=F32) + jnp.dot(sel, mid, preferred_element_type=F32)
            + jnp.dot(sel, lo, preferred_element_type=F32))


def _scan_steps(a, b, row, length, axis=0):
    s = 1
    while s < length:
        keep = row >= s
        a_prev = jnp.where(keep, pltpu.roll(a, s, axis), 1.0)
        b_prev = jnp.where(keep, pltpu.roll(b, s, axis), 0.0)
        b = a * b_prev + b
        a = a * a_prev
        s *= 2
    return a, b


def _rglru_kernel(x_ref, g_ref, w_ref, cw_ref, cb_ref, wra_ref, bra_ref, wri_ref, bri_ref, lam_ref,
                  o_ref, ext_s, h_s, a_s, b_s):
    t = pl.program_id(1)

    @pl.when(t == 0)
    def _():
        ext_s[0:CONV_PAD, :] = jnp.zeros((CONV_PAD, WIDTH), F32)
        h_s[...] = jnp.zeros_like(h_s)

    @pl.when(t > 0)
    def _():
        ext_s[0:CONV_PAD, :] = ext_s[TS:TS + CONV_PAD, :]

    u = jnp.dot(_normed_input(x_ref, g_ref), w_ref[...], preferred_element_type=F32)
    ext_s[CONV_PAD:CONV_PAD + TS, :] = u[:, 0:WIDTH]
    cz = u[:, WIDTH:2 * WIDTH]
    cw = cw_ref[...]
    xc = cb_ref[...] + cw[CONV_WIDTH - 1:CONV_WIDTH, :] * u[:, 0:WIDTH]
    for j in range(CONV_WIDTH - 1):
        back = CONV_WIDTH - 1 - j
        xc = xc + cw[j:j + 1, :] * ext_s[CONV_PAD - back:CONV_PAD - back + TS, :]
    r = jax.nn.sigmoid(_dot(xc, wra_ref[...]) + bra_ref[...])
    ig = jax.nn.sigmoid(_dot(xc, wri_ref[...]) + bri_ref[...])
    neg_lam = -lam_ref[...]
    softplus = jnp.maximum(neg_lam, 0.0) + jnp.log1p(jnp.exp(-jnp.abs(neg_lam)))
    log_a = -RG_C * r * softplus
    a = jnp.exp(log_a)
    th = jnp.tanh(log_a)
    b = jnp.sqrt(-2.0 * th / (1.0 - th)) * (ig * xc)
    n_groups = TS // GROUP
    grouped = (n_groups, GROUP, WIDTH)
    a, b = _scan_steps(a.reshape(grouped), b.reshape(grouped), _iota(grouped, 1), GROUP, axis=1)
    a, b = a.reshape(TS, WIDTH), b.reshape(TS, WIDTH)
    last_rows = pl.ds(GROUP - 1, n_groups, stride=GROUP)
    for j in range(WIDTH // LANES):
        a_s[j] = a[:, j * LANES:(j + 1) * LANES]
        b_s[j] = b[:, j * LANES:(j + 1) * LANES]
    a_last = jnp.concatenate([a_s[j, last_rows, :] for j in range(WIDTH // LANES)], axis=1)
    b_last = jnp.concatenate([b_s[j, last_rows, :] for j in range(WIDTH // LANES)], axis=1)
    group = _iota((n_groups, WIDTH), 0)
    a_last, b_last = _scan_steps(a_last, b_last, group, n_groups)
    h0 = h_s[0:1, :]
    h_after = a_last * h0 + b_last
    h_before = jnp.where(group == 0, h0, pltpu.roll(h_after, 1, 0))
    h = a * _expand_groups(h_before) + b
    h_s[0:1, :] = h_after[n_groups - 1:n_groups, :]
    o_ref[...] = (h * _silu(cz)).astype(o_ref.dtype)


NSA_Q = 0
NSA_KC = 512
NSA_VC = 640
NSA_KS = 768
NSA_VS = 896
NSA_KW = 1024
NSA_VW = 1152
NSA_G = 1280
NSA_Z = 1408
NSA_COLS = 1920
SEL_KEYS = TS
WIN_Q = 256
WIN_KEYS = Q_BLOCK + WINDOW
CMP_PHASES = CMP_STRIDE
AUG = 2 * LANES
POS_ROWS = 16
MASK_BIG = float(2.0 ** 40)
SUM_ROWS = 16
SEL_COLS = 512
SEL_AHEAD = 2
T_ROWS = (6, 7, 8)
FIXED_SHIFT_RANGE = 60.0


def _half_rms(v, first_half):
    v2 = v * v
    s0 = jnp.sum(jnp.where(first_half, v2, 0.0), axis=-1, keepdims=True)
    s1 = jnp.sum(jnp.where(first_half, 0.0, v2), axis=-1, keepdims=True)
    ms = jnp.where(first_half, s0, s1) * (1.0 / B_HEAD_DIM)
    return v * lax.rsqrt(ms + EPS)


def _rank_select(imp, n_live):
    n_sel, n_query = imp.shape
    cnt = []
    for v0 in range(0, n_live, 8):
        mine, jmine = imp[v0:v0 + 8], v0 + _iota((8, n_query), 0)
        c = jnp.zeros((8, n_query), F32)
        for jp in range(n_live):
            other = imp[jp:jp + 1, :]
            if jp < v0:
                ahead = other >= mine
            elif jp >= v0 + 8:
                ahead = other > mine
            else:
                ahead = jnp.where(jmine > jp, jnp.where(other >= mine, 1.0, 0.0),
                                  jnp.where(other > mine, 1.0, 0.0)) > 0.5
            c = c + jnp.where(ahead, 1.0, 0.0)
        cnt.append(jnp.where(c < SEL_TOPN, 0.0, -MASK_BIG))
    if n_live < n_sel:
        cnt.append(jnp.full((n_sel - n_live, n_query), -MASK_BIG, F32))
    return jnp.concatenate(cnt, axis=0).astype(BF16)


def _softmax_cols(s):
    p = jnp.exp2(s - jnp.max(s, axis=0, keepdims=True))
    return p, jnp.sum(p, axis=0, keepdims=True)


def _nsa_kernel(seq, online, x_ref, g_ref, w_ref, qg_ref, kg_ref, pos12_ref, w12k_ref, w12v_ref,
                ov_ref, slope_ref, slopef_ref, possel_ref, poswin_ref, poscmp_ref, o_ref,
                u_s, ks_c, vst_c, kw_c, vwt_c, kc_c, vc_c, p1k_s, p1v_s, kv_s):
    n_cmp = seq // CMP_STRIDE
    n_sel = seq // SEL_LEN
    ti = pl.program_id(1)

    @pl.when(ti == 0)
    def _():
        for ref in (ks_c, vst_c, kw_c, vwt_c, kc_c, vc_c, p1k_s, p1v_s):
            ref[...] = jnp.zeros_like(ref)

    u_s[...] = jnp.dot(_normed_input(x_ref, g_ref), w_ref[...], preferred_element_type=F32)
    row0 = pl.multiple_of(ti * TS, TS)
    kg = kg_ref[...]
    half_t = _iota((TS, LANES), 1) < B_HEAD_DIM

    ks_c[pl.ds(row0, TS), :] = (_half_rms(u_s[:, NSA_KS:NSA_KS + B_KV], half_t) * kg).astype(BF16)
    vst_c[ti] = u_s[:, NSA_VS:NSA_VS + B_KV].T.astype(BF16)
    kw_c[pl.ds(row0 + WINDOW, TS), :] = (_half_rms(u_s[:, NSA_KW:NSA_KW + B_KV], half_t) * kg).astype(BF16)
    for i in range(TS // Q_BLOCK):
        vwt_c[ti * (TS // Q_BLOCK) + WINDOW // Q_BLOCK + i] = (
            u_s[i * Q_BLOCK:(i + 1) * Q_BLOCK, NSA_VW:NSA_VW + B_KV].T.astype(BF16))

    n_grp = TS // CMP_STRIDE
    first_row = _iota((n_grp, B_KV), 0) == 0
    half_g = _iota((n_grp, LANES), 1) < B_HEAD_DIM
    crow = pl.ds(pl.multiple_of(ti * n_grp, n_grp), n_grp)
    for i, (col, w12_ref, p1_s, cache, is_key) in enumerate(((NSA_KC, w12k_ref, p1k_s, kc_c, True),
                                                             (NSA_VC, w12v_ref, p1v_s, vc_c, False))):
        kv_s[i] = u_s[:, col:col + B_KV]
        grouped = jnp.concatenate([kv_s[i, pl.ds(r, n_grp, stride=CMP_STRIDE), :] for r in range(CMP_PHASES)],
                                  axis=1)
        both = _dot(jnp.concatenate([grouped, pos12_ref[...]], axis=0), w12_ref[...])
        p1 = both[0:n_grp, 0:B_KV] + both[n_grp:n_grp + 1, 0:B_KV]
        p2 = both[0:n_grp, B_KV:2 * B_KV] + both[n_grp + 1:n_grp + 2, B_KV:2 * B_KV]
        p1_prev = jnp.where(first_row, p1_s[n_grp - 1:n_grp, :], pltpu.roll(p1, 1, 0))
        p1_s[...] = p1
        blk = p1_prev + p2
        if is_key:
            blk = _half_rms(blk, half_g) * kg
        cache[crow, :] = blk.astype(BF16)

    qg = qg_ref[...]
    n_q = B_REP * TS
    halves = TS // WIN_Q

    def cols(a):
        return jnp.concatenate([a[:, hb * WIN_Q:(hb + 1) * WIN_Q] for hb in range(halves) for _ in range(B_REP)],
                               axis=1)

    def col_block(hb, r):
        return slice((hb * B_REP + r) * WIN_Q, (hb * B_REP + r + 1) * WIN_Q)

    q_n = [_half_rms(u_s[:, NSA_Q + r * LANES:NSA_Q + (r + 1) * LANES], half_t)
           * qg[:, r * LANES:(r + 1) * LANES] * (B_HEAD_DIM ** -0.5 * LOG2E) for r in range(B_REP)]
    q_t = jnp.concatenate([q_n[r][hb * WIN_Q:(hb + 1) * WIN_Q].T for hb in range(halves) for r in range(B_REP)],
                          axis=1)
    q_rows_first = _iota((LANES, n_q), 0) < B_HEAD_DIM
    t_col = (row0 + cols(_iota((1, TS), 1))).astype(F32)
    prow = _iota((POS_ROWS, n_q), 0)
    slope_rows = []
    for g in range(B_GROUPS):
        rest = -slopef_ref[g] * t_col
        block = slope_ref[g]
        for row in T_ROWS:
            piece = rest.astype(BF16).astype(F32)
            block = block + jnp.where(prow == row, piece, 0.0)
            rest = rest - piece
        slope_rows.append(block.astype(BF16))
    base = [jnp.concatenate([jnp.where(q_rows_first, q_t, 0.0).astype(BF16), slope_rows[0]], axis=0),
            jnp.concatenate([jnp.where(q_rows_first, 0.0, q_t).astype(BF16), slope_rows[1]], axis=0)]
    n_base = LANES + POS_ROWS
    groups = range(B_GROUPS)

    t_c = row0 + _iota((n_cmp, TS), 1)
    nprime = _iota((n_cmp, TS), 0)
    madd_c = cols(jnp.where(t_c >= nprime * CMP_STRIDE + (CMP_STRIDE - 1),
                            jnp.where(nprime >= 1, 0.0, NEG_INF), NEG_INF))
    any_c = cols(jnp.where(row0 + _iota((1, TS), 1) >= CMP_LEN - 1, 1.0, 0.0))
    kaug_c = jnp.concatenate([kc_c[...], poscmp_ref[...]], axis=1)
    vct = vc_c[...].astype(F32).T.astype(BF16)
    zeros_c = jnp.zeros((AUG - n_base, n_q), BF16)
    s_c = [jnp.dot(kaug_c, jnp.concatenate([base[g], zeros_c], axis=0), preferred_element_type=F32) + madd_c
           for g in groups]
    p_c = []
    for g in groups:
        p, l = _softmax_cols(s_c[g])
        p_c.append(p * (any_c / l))
    o_cmp = [_dot(vct[g * B_HEAD_DIM:(g + 1) * B_HEAD_DIM], p_c[g]) for g in groups]
    blk_t = (row0 + _iota((n_sel, TS), 1)) >> 6
    jrow = _iota((n_sel, TS), 0)
    q_aug = []
    for g in groups:
        p_sum = jnp.concatenate(
            [sum(p_c[g][:, col_block(hb, r)] for r in range(B_REP)) for hb in range(halves)], axis=1)
        imp = jnp.dot(ov_ref[...], p_sum, precision=lax.Precision.HIGHEST,
                      preferred_element_type=F32)
        imp = jnp.where(jrow == blk_t, BIG, jnp.where(jrow < blk_t, imp, -BIG))
        sel_neg = lax.switch(ti, [functools.partial(_rank_select, n_live=(k + 1) * (TS // SEL_LEN))
                                  for k in range(seq // TS)], imp)
        q_aug.append(jnp.concatenate([base[g], cols(sel_neg),
                                      jnp.zeros((AUG - n_base - n_sel, n_q), BF16)], axis=0))

    def sel_step(kc, st, diagonal):
        k0 = pl.multiple_of(kc * SEL_KEYS, SEL_KEYS)
        kaug = jnp.concatenate([ks_c[pl.ds(k0, SEL_KEYS), :], possel_ref[pl.ds(k0, SEL_KEYS), :]], axis=1)
        ones = jnp.ones((SUM_ROWS, SEL_KEYS), BF16)
        v_aug = [jnp.concatenate([vst_c[kc, g * B_HEAD_DIM:(g + 1) * B_HEAD_DIM, :], ones], axis=0) for g in groups]
        if diagonal:
            madd = cols(jnp.where(_iota((SEL_KEYS, TS), 0) <= _iota((SEL_KEYS, TS), 1), 0.0, NEG_INF))
        units = [(g, slice(b * SEL_COLS, (b + 1) * SEL_COLS)) for b in range(n_q // SEL_COLS) for g in groups]

        def n_keys(cs):
            return (cs.start // (B_REP * WIN_Q) + 1) * WIN_Q if diagonal else SEL_KEYS

        def qk(g, cs):
            n = n_keys(cs)
            s = jnp.dot(kaug[0:n], q_aug[g][:, cs], preferred_element_type=F32)
            return s + madd[0:n, cs] if diagonal else s

        scores = [qk(*u) for u in units[:SEL_AHEAD]]
        m_out, acc_out = [[] for _ in groups], [[] for _ in groups]
        for i, (g, cs) in enumerate(units):
            if i + SEL_AHEAD < len(units):
                scores.append(qk(*units[i + SEL_AHEAD]))
            s = scores[i]
            m_old, acc_old = st[2 * g][:, cs], st[2 * g + 1][:, cs]
            if online:
                m_new = jnp.maximum(m_old, jnp.max(s, axis=0, keepdims=True))
                s = s - m_new
                acc_old = jnp.exp2(m_old - m_new) * acc_old
            else:
                m_new = m_old
            p = jnp.exp2(s.astype(BF16))
            m_out[g].append(m_new)
            acc_out[g].append(acc_old + jnp.dot(v_aug[g][:, 0:n_keys(cs)], p, preferred_element_type=F32))
        out = []
        for g in groups:
            out += [jnp.concatenate(m_out[g], axis=1), jnp.concatenate(acc_out[g], axis=1)]
        return tuple(out)

    init = []
    for g in groups:
        init += [jnp.full((1, n_q), NEG_INF, F32), jnp.zeros((B_HEAD_DIM + SUM_ROWS, n_q), F32)]
    st = lax.fori_loop(0, ti, functools.partial(sel_step, diagonal=False), tuple(init))
    st = sel_step(ti, st, diagonal=True)
    o_sel = [st[2 * g + 1][0:B_HEAD_DIM] / st[2 * g + 1][B_HEAD_DIM:B_HEAD_DIM + 1] for g in groups]

    per_half = WIN_Q // Q_BLOCK
    o_win_pass = []
    for qi in range(TS // Q_BLOCK):
        hb, qq = divmod(qi, per_half)
        w0 = row0 + qi * Q_BLOCK
        kaug_w = jnp.concatenate([kw_c[pl.ds(w0, WIN_KEYS), :], poswin_ref[pl.ds(w0, WIN_KEYS), :]], axis=1)
        wblk = ti * (TS // Q_BLOCK) + qi
        ones = jnp.ones((SUM_ROWS, WIN_KEYS), BF16)
        vwt = [jnp.concatenate(
            [jnp.concatenate([vwt_c[wblk + i, g * B_HEAD_DIM:(g + 1) * B_HEAD_DIM, :]
                              for i in range(WIN_KEYS // Q_BLOCK)], axis=1), ones], axis=0) for g in groups]
        spos = (w0 - WINDOW) + _iota((WIN_KEYS, Q_BLOCK), 0)
        dist_w = (w0 + _iota((WIN_KEYS, Q_BLOCK), 1)) - spos
        madd_w = jnp.where(dist_w >= 0, jnp.where(dist_w < WINDOW, jnp.where(spos >= 0, 0.0, NEG_INF), NEG_INF),
                           NEG_INF)
        madd_w = jnp.concatenate([madd_w] * B_REP, axis=1)
        q_w = [jnp.concatenate([q_aug[g][:, col_block(hb, r).start + qq * Q_BLOCK:
                                         col_block(hb, r).start + (qq + 1) * Q_BLOCK] for r in range(B_REP)], axis=1)
               for g in groups]
        s_w = [jnp.dot(kaug_w, q_w[g], preferred_element_type=F32) + madd_w for g in groups]
        if online:
            s_w = [s - jnp.max(s, axis=0, keepdims=True) for s in s_w]
        p_w = [jnp.exp2(s.astype(BF16)) for s in s_w]
        pv = [jnp.dot(vwt[g], p_w[g], preferred_element_type=F32) for g in groups]
        o_win_pass.append([x[0:B_HEAD_DIM] / x[B_HEAD_DIM:B_HEAD_DIM + 1] for x in pv])
    o_win = [jnp.concatenate([o_win_pass[hb * per_half + qq][g][:, r * Q_BLOCK:(r + 1) * Q_BLOCK]
                              for hb in range(halves) for r in range(B_REP) for qq in range(per_half)], axis=1)
             for g in groups]

    gates_t = jax.nn.sigmoid(u_s[:, NSA_G:NSA_G + LANES]).T
    for hb in range(halves):
        ts = slice(hb * WIN_Q, (hb + 1) * WIN_Q)
        for r in range(B_REP):
            cs = col_block(hb, r)
            acc = jnp.zeros((LANES, WIN_Q), F32)
            for c, branch in enumerate((o_cmp, o_sel, o_win)):
                acc = acc + jnp.concatenate(
                    [branch[0][:, cs] * gates_t[r * 3 + c:r * 3 + c + 1, ts],
                     branch[1][:, cs] * gates_t[(B_REP + r) * 3 + c:(B_REP + r) * 3 + c + 1, ts]], axis=0)
            z = u_s[ts, NSA_Z + r * LANES:NSA_Z + (r + 1) * LANES]
            o_ref[ts, r * LANES:(r + 1) * LANES] = (acc.T * _silu(z)).astype(o_ref.dtype)


def _merge_kernel(x_ref, oa_ref, ob_ref, oc_ref, od_ref, g_ref, wmg_ref, mb_ref, wbr_ref, wout_ref, o_ref):
    x = x_ref[...]
    xn = _rms_rows(x, g_ref[...]).astype(BF16)
    merged = jnp.zeros((TS, D_MODEL), F32)
    for br, o_k in enumerate((oa_ref, ob_ref, oc_ref, od_ref)):
        cols = slice(br * D_MODEL, (br + 1) * D_MODEL)
        gate = jax.nn.sigmoid(jnp.dot(xn, wmg_ref[:, cols], preferred_element_type=F32) + mb_ref[:, cols])
        merged = merged + gate * jnp.dot(o_k[...], wbr_ref[br], preferred_element_type=F32)
    o_ref[...] = x + jnp.dot(merged.astype(BF16), wout_ref[...], preferred_element_type=F32)


def _const_spec(shape):
    nd = len(shape)
    return pl.BlockSpec(shape, lambda b, t: (0,) * nd, pipeline_mode=pl.Buffered(1))


def _tile_spec(width):
    return pl.BlockSpec((None, TS, width), lambda b, t: (b, t, 0))


def _call(kernel, name, x, consts, out_width, out_dtype, scratch, extra_tiles=()):
    batch, seq, _ = x.shape
    in_specs = [_tile_spec(D_MODEL)] + [_tile_spec(a.shape[-1]) for a in extra_tiles]
    in_specs += [_const_spec(c.shape) for c in consts]
    return pl.pallas_call(
        kernel,
        name=name,
        grid=(batch, seq // TS),
        in_specs=in_specs,
        out_specs=_tile_spec(out_width),
        out_shape=jax.ShapeDtypeStruct((batch, seq, out_width), out_dtype),
        scratch_shapes=scratch,
        compiler_params=pltpu.CompilerParams(
            dimension_semantics=("arbitrary", "arbitrary"), vmem_limit_bytes=VMEM_LIMIT),
    )(x, *extra_tiles, *consts)


def _retention_tables():
    pos = np.arange(TS, dtype=np.float64)
    log_g = np.log1p(-np.exp2(-5.0 - np.arange(D_HEADS, dtype=np.float64)))
    rel = pos[:, None] - pos[None, :]
    same = (pos[:, None] // CHUNK) == (pos[None, :] // CHUNK)
    dmat = np.where((rel >= 0) & same, np.exp(log_g[:, None, None] * np.maximum(rel, 0.0)), 0.0)
    dmat = dmat[:, :INTRA, :INTRA]
    inpos = pos % CHUNK
    kdec = np.repeat(np.exp(log_g[None, :] * (CHUNK - 1.0 - inpos)[:, None]), D_QK, axis=1)
    qdec = np.repeat(np.exp(log_g[None, :] * (inpos + 1.0)[:, None]), D_V, axis=1)
    cdec = np.repeat(np.exp(log_g * CHUNK)[None, :], D_V, axis=1)
    return tuple(jnp.asarray(a, F32) for a in (dmat, kdec, qdec, cdec))


def _bf16_pieces(v, n):
    out = []
    rem = np.asarray(v, np.float64)
    for _ in range(n):
        piece = rem.astype(BF16).astype(np.float64)
        out.append(piece)
        rem = rem - piece
    return out


def _position_tile(pos, block_ids=None):
    pos = np.maximum(np.asarray(pos), 0)
    tab = np.zeros((pos.shape[0], LANES), np.float32)
    for i in range(3):
        tab[:, 2 * i] = (pos // 256) * 256
        tab[:, 2 * i + 1] = pos % 256
    tab[:, list(T_ROWS)] = 1.0
    if block_ids is not None:
        tab[np.arange(pos.shape[0]), POS_ROWS + np.asarray(block_ids)] = 1.0
    return jnp.asarray(tab, BF16)


def _nsa_tables(seq):
    n_cmp = seq // CMP_STRIDE
    n_sel = seq // SEL_LEN
    n = np.arange(n_cmp) - 1
    cmp_start = n * CMP_STRIDE
    sel_start = np.arange(n_sel) * SEL_LEN
    ov = ((cmp_start[None, :] < sel_start[:, None] + SEL_LEN) & (cmp_start[None, :] + CMP_LEN > sel_start[:, None])
          & (n[None, :] >= 0) & (n[None, :] <= (seq - CMP_LEN) // CMP_STRIDE))
    slope = np.zeros((B_GROUPS, POS_ROWS, TS // WIN_Q, B_REP, WIN_Q), np.float32)
    for g in range(B_GROUPS):
        for r in range(B_REP):
            pieces = _bf16_pieces(B_SLOPES[g * B_REP + r] * LOG2E, 3)
            for i, piece in enumerate(pieces):
                slope[g, 2 * i:2 * i + 2, :, r, :] = piece
    slope = slope.reshape(B_GROUPS, POS_ROWS, B_REP * TS)
    slopef = np.zeros((B_GROUPS, 1, TS // WIN_Q, B_REP, WIN_Q), np.float32)
    for g in range(B_GROUPS):
        for r in range(B_REP):
            slopef[g, 0, :, r, :] = B_SLOPES[g * B_REP + r] * LOG2E
    slopef = slopef.reshape(B_GROUPS, 1, B_REP * TS)
    tpos = np.arange(seq)
    possel = _position_tile(tpos, tpos // SEL_LEN)
    poswin = _position_tile(np.arange(seq + WINDOW) - WINDOW)
    poscmp = _position_tile(np.arange(n_cmp) * CMP_STRIDE + (CMP_STRIDE - 1))
    return (jnp.asarray(ov, F32), jnp.asarray(slope, F32), jnp.asarray(slopef), possel, poswin, poscmp)


def _perm_heads(w):
    lead = w.shape[:-1]
    return w.reshape(lead + (B_HEADS, B_HEAD_DIM))[..., B_HEAD_PERM, :].reshape(lead + (WIDTH,))


def _cmp_weights(w):
    w = w.reshape(2, CMP_PHASES, B_HEAD_DIM, B_HEAD_DIM)
    eye = jnp.eye(B_GROUPS, dtype=w.dtype)
    big = jnp.einsum('spde,gh->spgdhe', w, eye).reshape(2, CMP_PHASES * B_KV, B_KV)
    return jnp.concatenate([big[0], big[1]], axis=1).astype(BF16)


def _cmp_pos(pos):
    p = jnp.tile(pos.reshape(2, CMP_PHASES, 1, B_HEAD_DIM), (1, 1, B_GROUPS, 1)).reshape(2, CMP_PHASES * B_KV)
    return jnp.concatenate([p, jnp.zeros((14, CMP_PHASES * B_KV), p.dtype)], axis=0)


def _block_diag(w):
    eye = jnp.eye(C_BLOCKS, dtype=w.dtype)
    return jnp.einsum('ncd,nm->ncmd', w, eye).reshape(WIDTH, WIDTH).astype(BF16)


def kernel(x, norm_g, w_in, lb_logits, a_norm_g, b_q_norm_g, b_k_norm_g, b_cmp_pos, b_cmp_wk, b_cmp_wv,
           c_conv_w, c_conv_b, c_w_ra, c_b_ra, c_w_ri, c_b_ri, c_lambda, d_norm_g, merge_b, w_branch, w_out):
    batch, seq, _ = x.shape
    depth = norm_g.shape[0]
    assert seq % TS == 0 and x.shape[-1] == D_MODEL

    p_lb = jax.nn.softmax(lb_logits.astype(F32), axis=0)
    lower_bounds = jnp.cumsum(p_lb, axis=0) - p_lb[0:1]
    dmat, kdec, qdec, cdec = _retention_tables()
    nsa_tabs = _nsa_tables(seq)
    row = lambda v: v.reshape(1, -1).astype(F32)
    vm = lambda shape, dt=F32: pltpu.VMEM(shape, dt)
    n_cmp = seq // CMP_STRIDE

    for l in range(depth):
        w = w_in[l]
        g = row(norm_g[l])
        w_a = w[:, 0:2048].astype(BF16)
        b0 = 2048
        gate_cols = jnp.pad(w[:, b0 + 1280:b0 + 1304], ((0, 0), (0, LANES - 3 * B_HEADS)))
        w_b = jnp.concatenate([_perm_heads(w[:, b0:b0 + 512]), w[:, b0 + 512:b0 + 1280], gate_cols,
                               _perm_heads(w[:, b0 + 1304:b0 + 1816])], axis=1).astype(BF16)
        c0 = b0 + 1816
        w_c = w[:, c0:c0 + 1024].astype(BF16)
        d0 = c0 + 1024
        w_d = w[:, d0:d0 + 1536].astype(BF16)
        m0 = d0 + 1536
        w_m = w[:, m0:m0 + N_BRANCH * D_MODEL].astype(BF16)

        o_a = _call(_hgrn_kernel, "hgrn", x, [g, w_a, row(lower_bounds[l]), row(a_norm_g[l])], WIDTH, BF16,
                    [vm((A_HEADS, A_HEAD_DIM, A_HEAD_DIM))])

        nsa_consts = [g, w_b, row(jnp.tile(b_q_norm_g[l], B_HEADS)), row(jnp.tile(b_k_norm_g[l], B_GROUPS)),
                      _cmp_pos(b_cmp_pos[l]), _cmp_weights(b_cmp_wk[l]), _cmp_weights(b_cmp_wv[l]), *nsa_tabs]
        nsa_scratch = [vm((TS, NSA_COLS)), vm((seq, B_KV), BF16), vm((seq // TS, B_KV, TS), BF16),
                       vm((seq + WINDOW, B_KV), BF16), vm(((seq + WINDOW) // Q_BLOCK, B_KV, Q_BLOCK), BF16),
                       vm((n_cmp, B_KV), BF16), vm((n_cmp, B_KV), BF16),
                       vm((TS // CMP_STRIDE, B_KV)), vm((TS // CMP_STRIDE, B_KV)), vm((2, TS, B_KV))]
        nsa = lambda online: _call(functools.partial(_nsa_kernel, seq, online), "nsa_online" if online else "nsa",
                                   x, nsa_consts, WIDTH, BF16, nsa_scratch)
        score_bound = (B_HEAD_DIM * B_HEAD_DIM ** -0.5 * LOG2E * jnp.max(jnp.abs(b_q_norm_g[l]))
                       * jnp.max(jnp.abs(b_k_norm_g[l])))
        o_b = lax.cond(score_bound <= FIXED_SHIFT_RANGE, lambda: nsa(False), lambda: nsa(True))

        o_c = _call(_rglru_kernel, "rglru", x,
                    [g, w_c, c_conv_w[l].astype(F32), row(c_conv_b[l]), _block_diag(c_w_ra[l]), row(c_b_ra[l]),
                     _block_diag(c_w_ri[l]), row(c_b_ri[l]), row(c_lambda[l])], WIDTH, BF16,
                    [vm((TS + 2 * CONV_PAD, WIDTH)), vm((8, WIDTH)), vm((WIDTH // LANES, TS, LANES)),
                     vm((WIDTH // LANES, TS, LANES))])

        o_d = _call(_ret_kernel, "retention", x, [g, w_d, dmat, kdec, qdec, cdec, row(d_norm_g[l])], WIDTH, BF16,
                    [vm((D_HEADS, LANES, D_V))])

        wbr = w_branch[l]
        wbr = jnp.stack([wbr[0], _perm_heads(wbr[1].T).T, wbr[2], wbr[3]], axis=0).astype(BF16)
        x = _call(_merge_kernel, "merge", x, [g, w_m, row(merge_b[l]), wbr, w_out[l].astype(BF16)],
                  D_MODEL, x.dtype, [], extra_tiles=(o_a, o_b, o_c, o_d))
    return x
```

```python
import functools

import numpy as np
import jax
import jax.numpy as jnp
from jax import lax
from jax.experimental import pallas as pl
from jax.experimental.pallas import tpu as pltpu

F32 = jnp.float32
BF16 = jnp.bfloat16

D_MODEL = 1024
WIDTH = 512
EPS = 1e-6
NEG_INF = -1e30
BIG = 1e30

A_HEADS = 4
A_HEAD_DIM = 128
CHUNK = 64

B_HEADS = 8
B_GROUPS = 2
B_REP = B_HEADS // B_GROUPS
B_HEAD_DIM = 64
B_KV = B_GROUPS * B_HEAD_DIM
CMP_LEN = 32
CMP_STRIDE = 16
SEL_LEN = 64
SEL_TOPN = 8
WINDOW = 256
Q_BLOCK = 128

C_BLOCKS = 8
C_BLOCK_DIM = 64
CONV_WIDTH = 4
RG_C = 8.0

D_HEADS = 4
D_QK = 64
D_V = 128

N_BRANCH = 4

TS = 512
LANES = 128
VMEM_LIMIT = 56 * 1024 * 1024
LOG2E = 1.4426950408889634

B_HEAD_PERM = (0, 4, 1, 5, 2, 6, 3, 7)
B_SLOPES = tuple(float(2.0 ** (-8.0 * (h + 1) / B_HEADS)) for h in range(B_HEADS))


def _dot(a, b):
    return jnp.dot(a.astype(BF16), b.astype(BF16), preferred_element_type=F32)


def _dot_nt(a, b):
    return lax.dot_general(a.astype(BF16), b.astype(BF16), (((1,), (1,)), ((), ())),
                           preferred_element_type=F32)


def _dot_tn(a, b):
    return lax.dot_general(a.astype(BF16), b.astype(BF16), (((0,), (0,)), ((), ())),
                           preferred_element_type=F32)


def _rms_rows(x, g):
    ms = jnp.mean(x * x, axis=-1, keepdims=True)
    return x * lax.rsqrt(ms + EPS) * g


def _sigmoid(x):
    return 0.5 * jnp.tanh(0.5 * x) + 0.5


def _silu(z):
    return z * _sigmoid(z)


def _iota(shape, dim):
    return lax.broadcasted_iota(jnp.int32, shape, dim)


def _normed_input(x_ref, g_ref):
    return _rms_rows(x_ref[...], g_ref[...]).astype(BF16)


N_CHUNKS = TS // CHUNK
INTRA = 256


def _chunk_broadcast(v, row):
    v3 = v.reshape(N_CHUNKS, CHUNK, v.shape[-1])
    return jnp.broadcast_to(v3[:, row:row + 1, :], v3.shape).reshape(v.shape)


def _chunk_rows(v, row):
    return v.reshape(N_CHUNKS, CHUNK, v.shape[-1])[:, row, :]


def _chunk_block_diag(v):
    n = v.shape[0] // CHUNK
    zero = jnp.zeros((CHUNK, v.shape[-1]), v.dtype)
    return jnp.concatenate(
        [jnp.concatenate([zero] * c + [v[c * CHUNK:(c + 1) * CHUNK]] + [zero] * (n - 1 - c), axis=1)
         for c in range(n)], axis=0)


def _chunk_diag_blocks(v, width):
    n = v.shape[0] // CHUNK
    return jnp.concatenate([v[c * CHUNK:(c + 1) * CHUNK, c * width:(c + 1) * width] for c in range(n)], axis=0)


def _intra_rows(i):
    return slice(i * INTRA, (i + 1) * INTRA)


def _hgrn_kernel(x_ref, g_ref, w_ref, lb_ref, ng_ref, o_ref, st_s):
    @pl.when(pl.program_id(1) == 0)
    def _():
        st_s[...] = jnp.zeros_like(st_s)

    u = jnp.dot(_normed_input(x_ref, g_ref), w_ref[...], preferred_element_type=F32)
    q = u[:, 0:WIDTH]
    fl = u[:, WIDTH:2 * WIDTH]
    lb = lb_ref[...]
    f = lb + (1.0 - lb) * jax.nn.sigmoid(fl)
    k = (1.0 - lb) * jax.nn.sigmoid(-fl)
    vb = u[:, 2 * WIDTH:3 * WIDTH].astype(BF16)
    b = jnp.log(f)
    row_in_chunk = _iota((TS, WIDTH), 0) & (CHUNK - 1)
    s = 1
    while s < CHUNK:
        b = b + jnp.where(row_in_chunk >= s, pltpu.roll(b, s, 0), 0.0)
        s *= 2
    b_mid = _chunk_broadcast(b, CHUNK // 2 - 1)
    b_end = _chunk_broadcast(b, CHUNK - 1)
    qe = (q * jnp.exp(b - b_mid)).astype(BF16)
    ke = (k * jnp.exp(b_mid - b)).astype(BF16)
    k_end = (k * jnp.exp(b_end - b)).astype(BF16)
    qb = (q * jnp.exp(b)).astype(BF16)
    dec = jnp.exp(_chunk_rows(b, CHUNK - 1))

    ri, ci = _iota((INTRA, INTRA), 0), _iota((INTRA, INTRA), 1)
    block_causal = jnp.logical_and(ri >= ci, (ri >> 6) == (ci >> 6))
    ng = ng_ref[...]
    for h in range(A_HEADS):
        sl = slice(h * A_HEAD_DIM, (h + 1) * A_HEAD_DIM)
        intra = []
        for i in range(TS // INTRA):
            rows = slice(i * INTRA, (i + 1) * INTRA)
            sc = jnp.where(block_causal, _dot_nt(qe[rows, sl], ke[rows, sl]), 0.0)
            intra.append(_dot(sc, vb[rows, sl]))
        kv_all = jnp.concatenate([_dot_tn(vb[_intra_rows(i), sl], _chunk_block_diag(k_end[_intra_rows(i), sl]))
                                  for i in range(TS // INTRA)], axis=1)
        st = st_s[h]
        states = []
        for c in range(N_CHUNKS):
            states.append(st.astype(BF16))
            st = st * dec[c:c + 1, sl] + kv_all[:, c * A_HEAD_DIM:(c + 1) * A_HEAD_DIM]
        st_s[h] = st
        per = INTRA // CHUNK
        inter = jnp.concatenate(
            [_chunk_diag_blocks(_dot_nt(qb[_intra_rows(i), sl], jnp.concatenate(states[i * per:(i + 1) * per], axis=0)),
                                A_HEAD_DIM) for i in range(TS // INTRA)], axis=0)
        o = jnp.concatenate(intra, axis=0) + inter
        z = u[:, 3 * WIDTH + h * A_HEAD_DIM:3 * WIDTH + (h + 1) * A_HEAD_DIM]
        o_ref[:, sl] = (_rms_rows(o, ng[:, sl]) * _silu(z)).astype(o_ref.dtype)


def _ret_kernel(x_ref, g_ref, w_ref, dmat_ref, kdec_ref, qdec_ref, cdec_ref, ng_ref, o_ref, st_s):
    @pl.when(pl.program_id(1) == 0)
    def _():
        st_s[...] = jnp.zeros_like(st_s)

    u = jnp.dot(_normed_input(x_ref, g_ref), w_ref[...], preferred_element_type=F32)
    nqk = D_HEADS * D_QK
    q = u[:, 0:nqk] * (D_QK ** -0.5)
    k = u[:, nqk:2 * nqk]
    kb = k.astype(BF16)
    kd = (k * kdec_ref[...]).astype(BF16)
    vb = u[:, 2 * nqk:2 * nqk + WIDTH].astype(BF16)
    first_half = _iota((TS, LANES), 1) < D_QK
    qdec = qdec_ref[...]
    cdec = cdec_ref[...]
    ng = ng_ref[...]
    for h in range(D_HEADS):
        pair = slice((h // 2) * LANES, (h // 2 + 1) * LANES)
        vsl = slice(h * D_V, (h + 1) * D_V)
        qm = jnp.where(first_half if h % 2 == 0 else jnp.logical_not(first_half), q[:, pair], 0.0).astype(BF16)
        intra = []
        for i in range(TS // INTRA):
            rows = slice(i * INTRA, (i + 1) * INTRA)
            sc = _dot_nt(qm[rows], kb[rows, pair]) * dmat_ref[h]
            intra.append(_dot(sc, vb[rows, vsl]))
        kv_all = jnp.concatenate([_dot_tn(kd[_intra_rows(i), pair], _chunk_block_diag(vb[_intra_rows(i), vsl]))
                                  for i in range(TS // INTRA)], axis=1)
        st = st_s[h]
        states = []
        for c in range(N_CHUNKS):
            states.append(st.astype(BF16))
            st = st * cdec[:, vsl] + kv_all[:, c * D_V:(c + 1) * D_V]
        st_s[h] = st
        per = INTRA // CHUNK
        inter = jnp.concatenate(
            [_chunk_diag_blocks(_dot(qm[_intra_rows(i)], jnp.concatenate(states[i * per:(i + 1) * per], axis=1)), D_V)
             for i in range(TS // INTRA)], axis=0)
        o = jnp.concatenate(intra, axis=0) + inter * qdec[:, vsl]
        oc = o - jnp.mean(o, axis=-1, keepdims=True)
        var = jnp.mean(oc * oc, axis=-1, keepdims=True)
        z = u[:, 2 * nqk + WIDTH + h * D_V:2 * nqk + WIDTH + (h + 1) * D_V]
        o_ref[:, vsl] = (oc * lax.rsqrt(var + EPS) * ng[:, vsl] * _silu(z)).astype(o_ref.dtype)


CONV_PAD = 8
GROUP = 8


def _scan_steps(a, b, row, length):
    s = 1
    while s < length:
        keep = row >= s
        a_prev = jnp.where(keep, pltpu.roll(a, s, 0), 1.0)
        b_prev = jnp.where(keep, pltpu.roll(b, s, 0), 0.0)
        b = a * b_prev + b
        a = a * a_prev
        s *= 2
    return a, b


def _rglru_kernel(x_ref, g_ref, w_ref, cw_ref, cb_ref, wra_ref, bra_ref, wri_ref, bri_ref, lam_ref,
                  o_ref, ext_s, h_s, a_s, b_s, ca_s, cb_s):
    t = pl.program_id(1)

    @pl.when(t == 0)
    def _():
        ext_s[0:CONV_PAD, :] = jnp.zeros((CONV_PAD, WIDTH), F32)
        h_s[...] = jnp.zeros_like(h_s)

    @pl.when(t > 0)
    def _():
        ext_s[0:CONV_PAD, :] = ext_s[TS:TS + CONV_PAD, :]

    u = jnp.dot(_normed_input(x_ref, g_ref), w_ref[...], preferred_element_type=F32)
    ext_s[CONV_PAD:CONV_PAD + TS, :] = u[:, 0:WIDTH]
    cz = u[:, WIDTH:2 * WIDTH]
    cw = cw_ref[...]
    xc = cb_ref[...] + cw[CONV_WIDTH - 1:CONV_WIDTH, :] * u[:, 0:WIDTH]
    for j in range(CONV_WIDTH - 1):
        back = CONV_WIDTH - 1 - j
        xc = xc + cw[j:j + 1, :] * ext_s[CONV_PAD - back:CONV_PAD - back + TS, :]
    r = _sigmoid(_dot(xc, wra_ref[...]) + bra_ref[...])
    ig = _sigmoid(_dot(xc, wri_ref[...]) + bri_ref[...])
    neg_lam = -lam_ref[...]
    softplus = jnp.maximum(neg_lam, 0.0) + jnp.log1p(jnp.exp(-jnp.abs(neg_lam)))
    log_a = -RG_C * r * softplus
    a = jnp.exp(log_a)
    b = jnp.sqrt(1.0 - a * a) * (ig * xc)
    n_groups = TS // GROUP
    planes = WIDTH // LANES
    rows_of = lambda r: pl.ds(r, n_groups, stride=GROUP)
    for j in range(planes):
        a_s[j] = a[:, j * LANES:(j + 1) * LANES]
        b_s[j] = b[:, j * LANES:(j + 1) * LANES]
    cum_a = [a_s[j, rows_of(0), :] for j in range(planes)]
    cum_b = [b_s[j, rows_of(0), :] for j in range(planes)]
    for r in range(GROUP):
        for j in range(planes):
            if r > 0:
                a_r = a_s[j, rows_of(r), :]
                cum_b[j] = a_r * cum_b[j] + b_s[j, rows_of(r), :]
                cum_a[j] = a_r * cum_a[j]
            ca_s[j, rows_of(r), :] = cum_a[j]
            cb_s[j, rows_of(r), :] = cum_b[j]
    group = _iota((n_groups, WIDTH), 0)
    a_tot, b_tot = _scan_steps(jnp.concatenate(cum_a, axis=1), jnp.concatenate(cum_b, axis=1), group, n_groups)
    h0 = h_s[0:1, :]
    h_after = a_tot * h0 + b_tot
    h_before = jnp.where(group == 0, h0, pltpu.roll(h_after, 1, 0))
    h_s[0:1, :] = h_after[n_groups - 1:n_groups, :]
    for r in range(GROUP):
        for j in range(planes):
            lanes = slice(j * LANES, (j + 1) * LANES)
            a_s[j, rows_of(r), :] = ca_s[j, rows_of(r), :] * h_before[:, lanes] + cb_s[j, rows_of(r), :]
    h = jnp.concatenate([a_s[j] for j in range(planes)], axis=1)
    o_ref[...] = (h * _silu(cz)).astype(o_ref.dtype)


NSA_Q = 0
NSA_KC = 512
NSA_VC = 640
NSA_KS = 768
NSA_VS = 896
NSA_KW = 1024
NSA_VW = 1152
NSA_G = 1280
NSA_Z = 1408
NSA_COLS = 1920
SEL_KEYS = TS
WIN_Q = 256
WIN_KEYS = Q_BLOCK + WINDOW
CMP_PHASES = CMP_STRIDE
AUG = 2 * LANES
POS_ROWS = 16
MASK_BIG = float(2.0 ** 40)
SUM_ROWS = 16
SEL_COLS = 512
SEL_AHEAD = 2
T_ROWS = (6, 7, 8)
FIXED_SHIFT_RANGE = 60.0


def _half_rms(v, first_half):
    v2 = v * v
    s0 = jnp.sum(jnp.where(first_half, v2, 0.0), axis=-1, keepdims=True)
    s1 = jnp.sum(jnp.where(first_half, 0.0, v2), axis=-1, keepdims=True)
    ms = jnp.where(first_half, s0, s1) * (1.0 / B_HEAD_DIM)
    return v * lax.rsqrt(ms + EPS)


def _rank_select(imp, n_live):
    n_sel, n_query = imp.shape
    cnt = []
    for v0 in range(0, n_live, 8):
        mine, jmine = imp[v0:v0 + 8], v0 + _iota((8, n_query), 0)
        c = jnp.zeros((8, n_query), F32)
        for jp in range(n_live):
            other = imp[jp:jp + 1, :]
            if jp < v0:
                ahead = other >= mine
            elif jp >= v0 + 8:
                ahead = other > mine
            else:
                ahead = jnp.where(jmine > jp, jnp.where(other >= mine, 1.0, 0.0),
                                  jnp.where(other > mine, 1.0, 0.0)) > 0.5
            c = c + jnp.where(ahead, 1.0, 0.0)
        cnt.append(jnp.where(c < SEL_TOPN, 0.0, -MASK_BIG))
    if n_live < n_sel:
        cnt.append(jnp.full((n_sel - n_live, n_query), -MASK_BIG, F32))
    return jnp.concatenate(cnt, axis=0).astype(BF16)


def _softmax_cols(s):
    p = jnp.exp2(s - jnp.max(s, axis=0, keepdims=True))
    return p, jnp.sum(p, axis=0, keepdims=True)


def _nsa_kernel(seq, online, x_ref, g_ref, w_ref, qg_ref, kg_ref, pos12_ref, w12k_ref, w12v_ref,
                ov_ref, slope_ref, slopef_ref, possel_ref, poswin_ref, poscmp_ref, o_ref,
                u_s, ks_c, vst_c, kw_c, vwt_c, kc_c, vc_c, p1k_s, p1v_s, kv_s):
    n_cmp = seq // CMP_STRIDE
    n_sel = seq // SEL_LEN
    ti = pl.program_id(1)

    @pl.when(ti == 0)
    def _():
        for ref in (ks_c, vst_c, kw_c, vwt_c, kc_c, vc_c, p1k_s, p1v_s):
            ref[...] = jnp.zeros_like(ref)

    u_s[...] = jnp.dot(_normed_input(x_ref, g_ref), w_ref[...], preferred_element_type=F32)
    row0 = pl.multiple_of(ti * TS, TS)
    kg = kg_ref[...]
    half_t = _iota((TS, LANES), 1) < B_HEAD_DIM

    ks_c[pl.ds(row0, TS), :] = (_half_rms(u_s[:, NSA_KS:NSA_KS + B_KV], half_t) * kg).astype(BF16)
    vst_c[ti] = u_s[:, NSA_VS:NSA_VS + B_KV].T.astype(BF16)
    kw_c[pl.ds(row0 + WINDOW, TS), :] = (_half_rms(u_s[:, NSA_KW:NSA_KW + B_KV], half_t) * kg).astype(BF16)
    for i in range(TS // Q_BLOCK):
        vwt_c[ti * (TS // Q_BLOCK) + WINDOW // Q_BLOCK + i] = (
            u_s[i * Q_BLOCK:(i + 1) * Q_BLOCK, NSA_VW:NSA_VW + B_KV].T.astype(BF16))

    n_grp = TS // CMP_STRIDE
    first_row = _iota((n_grp, B_KV), 0) == 0
    half_g = _iota((n_grp, LANES), 1) < B_HEAD_DIM
    crow = pl.ds(pl.multiple_of(ti * n_grp, n_grp), n_grp)
    for i, (col, w12_ref, p1_s, cache, is_key) in enumerate(((NSA_KC, w12k_ref, p1k_s, kc_c, True),
                                                             (NSA_VC, w12v_ref, p1v_s, vc_c, False))):
        kv_s[i] = u_s[:, col:col + B_KV]
        grouped = jnp.concatenate([kv_s[i, pl.ds(r, n_grp, stride=CMP_STRIDE), :] for r in range(CMP_PHASES)],
                                  axis=1)
        both = _dot(jnp.concatenate([grouped, pos12_ref[...]], axis=0), w12_ref[...])
        p1 = both[0:n_grp, 0:B_KV] + both[n_grp:n_grp + 1, 0:B_KV]
        p2 = both[0:n_grp, B_KV:2 * B_KV] + both[n_grp + 1:n_grp + 2, B_KV:2 * B_KV]
        p1_prev = jnp.where(first_row, p1_s[n_grp - 1:n_grp, :], pltpu.roll(p1, 1, 0))
        p1_s[...] = p1
        blk = p1_prev + p2
        if is_key:
            blk = _half_rms(blk, half_g) * kg
        cache[crow, :] = blk.astype(BF16)

    qg = qg_ref[...]
    n_q = B_REP * TS
    halves = TS // WIN_Q

    def cols(a):
        return jnp.concatenate([a[:, hb * WIN_Q:(hb + 1) * WIN_Q] for hb in range(halves) for _ in range(B_REP)],
                               axis=1)

    def col_block(hb, r):
        return slice((hb * B_REP + r) * WIN_Q, (hb * B_REP + r + 1) * WIN_Q)

    q_n = [_half_rms(u_s[:, NSA_Q + r * LANES:NSA_Q + (r + 1) * LANES], half_t)
           * qg[:, r * LANES:(r + 1) * LANES] * (B_HEAD_DIM ** -0.5 * LOG2E) for r in range(B_REP)]
    q_t = jnp.concatenate([q_n[r][hb * WIN_Q:(hb + 1) * WIN_Q].T for hb in range(halves) for r in range(B_REP)],
                          axis=1)
    q_rows_first = _iota((LANES, n_q), 0) < B_HEAD_DIM
    t_col = (row0 + cols(_iota((1, TS), 1))).astype(F32)
    prow = _iota((POS_ROWS, n_q), 0)
    slope_rows = []
    for g in range(B_GROUPS):
        rest = -slopef_ref[g] * t_col
        block = slope_ref[g]
        for row in T_ROWS:
            piece = rest.astype(BF16).astype(F32)
            block = block + jnp.where(prow == row, piece, 0.0)
            rest = rest - piece
        slope_rows.append(block.astype(BF16))
    base = [jnp.concatenate([jnp.where(q_rows_first, q_t, 0.0).astype(BF16), slope_rows[0]], axis=0),
            jnp.concatenate([jnp.where(q_rows_first, 0.0, q_t).astype(BF16), slope_rows[1]], axis=0)]
    n_base = LANES + POS_ROWS
    groups = range(B_GROUPS)

    t_c = row0 + _iota((n_cmp, TS), 1)
    nprime = _iota((n_cmp, TS), 0)
    madd_c = cols(jnp.where(t_c >= nprime * CMP_STRIDE + (CMP_STRIDE - 1),
                            jnp.where(nprime >= 1, 0.0, NEG_INF), NEG_INF))
    any_c = cols(jnp.where(row0 + _iota((1, TS), 1) >= CMP_LEN - 1, 1.0, 0.0))
    kaug_c = jnp.concatenate([kc_c[...], poscmp_ref[...]], axis=1)
    vct = vc_c[...].astype(F32).T.astype(BF16)
    zeros_c = jnp.zeros((AUG - n_base, n_q), BF16)
    s_c = [jnp.dot(kaug_c, jnp.concatenate([base[g], zeros_c], axis=0), preferred_element_type=F32) + madd_c
           for g in groups]
    p_c = []
    for g in groups:
        p, l = _softmax_cols(s_c[g])
        p_c.append(p * (any_c / l))
    o_cmp = [_dot(vct[g * B_HEAD_DIM:(g + 1) * B_HEAD_DIM], p_c[g]) for g in groups]
    blk_t = (row0 + _iota((n_sel, TS), 1)) >> 6
    jrow = _iota((n_sel, TS), 0)
    q_aug = []
    for g in groups:
        p_sum = jnp.concatenate(
            [sum(p_c[g][:, col_block(hb, r)] for r in range(B_REP)) for hb in range(halves)], axis=1)
        imp = jnp.dot(ov_ref[...], p_sum, precision=lax.Precision.HIGHEST,
                      preferred_element_type=F32)
        imp = jnp.where(jrow == blk_t, BIG, jnp.where(jrow < blk_t, imp, -BIG))
        sel_neg = lax.switch(ti, [functools.partial(_rank_select, n_live=(k + 1) * (TS // SEL_LEN))
                                  for k in range(seq // TS)], imp)
        q_aug.append(jnp.concatenate([base[g], cols(sel_neg),
                                      jnp.zeros((AUG - n_base - n_sel, n_q), BF16)], axis=0))

    def sel_step(kc, st, diagonal):
        k0 = pl.multiple_of(kc * SEL_KEYS, SEL_KEYS)
        kaug = jnp.concatenate([ks_c[pl.ds(k0, SEL_KEYS), :], possel_ref[pl.ds(k0, SEL_KEYS), :]], axis=1)
        ones = jnp.ones((SUM_ROWS, SEL_KEYS), BF16)
        v_aug = [jnp.concatenate([vst_c[kc, g * B_HEAD_DIM:(g + 1) * B_HEAD_DIM, :], ones], axis=0) for g in groups]
        if diagonal:
            madd = cols(jnp.where(_iota((SEL_KEYS, TS), 0) <= _iota((SEL_KEYS, TS), 1), 0.0, NEG_INF))
        units = [(g, slice(b * SEL_COLS, (b + 1) * SEL_COLS)) for b in range(n_q // SEL_COLS) for g in groups]

        def n_keys(cs):
            return (cs.start // (B_REP * WIN_Q) + 1) * WIN_Q if diagonal else SEL_KEYS

        def qk(g, cs):
            n = n_keys(cs)
            s = jnp.dot(kaug[0:n], q_aug[g][:, cs], preferred_element_type=F32)
            return s + madd[0:n, cs] if diagonal else s

        scores = [qk(*u) for u in units[:SEL_AHEAD]]
        m_out, acc_out = [[] for _ in groups], [[] for _ in groups]
        for i, (g, cs) in enumerate(units):
            if i + SEL_AHEAD < len(units):
                scores.append(qk(*units[i + SEL_AHEAD]))
            s = scores[i]
            m_old, acc_old = st[2 * g][:, cs], st[2 * g + 1][:, cs]
            if online:
                m_new = jnp.maximum(m_old, jnp.max(s, axis=0, keepdims=True))
                s = s - m_new
                acc_old = jnp.exp2(m_old - m_new) * acc_old
            else:
                m_new = m_old
            p = jnp.exp2(s.astype(BF16))
            m_out[g].append(m_new)
            acc_out[g].append(acc_old + jnp.dot(v_aug[g][:, 0:n_keys(cs)], p, preferred_element_type=F32))
        out = []
        for g in groups:
            out += [jnp.concatenate(m_out[g], axis=1), jnp.concatenate(acc_out[g], axis=1)]
        return tuple(out)

    init = []
    for g in groups:
        init += [jnp.full((1, n_q), NEG_INF, F32), jnp.zeros((B_HEAD_DIM + SUM_ROWS, n_q), F32)]
    st = lax.fori_loop(0, ti, functools.partial(sel_step, diagonal=False), tuple(init))
    st = sel_step(ti, st, diagonal=True)
    o_sel = [st[2 * g + 1][0:B_HEAD_DIM] / st[2 * g + 1][B_HEAD_DIM:B_HEAD_DIM + 1] for g in groups]

    per_half = WIN_Q // Q_BLOCK
    o_win_pass = []
    for qi in range(TS // Q_BLOCK):
        hb, qq = divmod(qi, per_half)
        w0 = row0 + qi * Q_BLOCK
        kaug_w = jnp.concatenate([kw_c[pl.ds(w0, WIN_KEYS), :], poswin_ref[pl.ds(w0, WIN_KEYS), :]], axis=1)
        wblk = ti * (TS // Q_BLOCK) + qi
        ones = jnp.ones((SUM_ROWS, WIN_KEYS), BF16)
        vwt = [jnp.concatenate(
            [jnp.concatenate([vwt_c[wblk + i, g * B_HEAD_DIM:(g + 1) * B_HEAD_DIM, :]
                              for i in range(WIN_KEYS // Q_BLOCK)], axis=1), ones], axis=0) for g in groups]
        spos = (w0 - WINDOW) + _iota((WIN_KEYS, Q_BLOCK), 0)
        dist_w = (w0 + _iota((WIN_KEYS, Q_BLOCK), 1)) - spos
        madd_w = jnp.where(dist_w >= 0, jnp.where(dist_w < WINDOW, jnp.where(spos >= 0, 0.0, NEG_INF), NEG_INF),
                           NEG_INF)
        madd_w = jnp.concatenate([madd_w] * B_REP, axis=1)
        q_w = [jnp.concatenate([q_aug[g][:, col_block(hb, r).start + qq * Q_BLOCK:
                                         col_block(hb, r).start + (qq + 1) * Q_BLOCK] for r in range(B_REP)], axis=1)
               for g in groups]
        s_w = [jnp.dot(kaug_w, q_w[g], preferred_element_type=F32) + madd_w for g in groups]
        if online:
            s_w = [s - jnp.max(s, axis=0, keepdims=True) for s in s_w]
        p_w = [jnp.exp2(s.astype(BF16)) for s in s_w]
        pv = [jnp.dot(vwt[g], p_w[g], preferred_element_type=F32) for g in groups]
        o_win_pass.append([x[0:B_HEAD_DIM] / x[B_HEAD_DIM:B_HEAD_DIM + 1] for x in pv])
    o_win = [jnp.concatenate([o_win_pass[hb * per_half + qq][g][:, r * Q_BLOCK:(r + 1) * Q_BLOCK]
                              for hb in range(halves) for r in range(B_REP) for qq in range(per_half)], axis=1)
             for g in groups]

    gates_t = _sigmoid(u_s[:, NSA_G:NSA_G + LANES]).T
    for hb in range(halves):
        ts = slice(hb * WIN_Q, (hb + 1) * WIN_Q)
        for r in range(B_REP):
            cs = col_block(hb, r)
            acc = jnp.zeros((LANES, WIN_Q), F32)
            for c, branch in enumerate((o_cmp, o_sel, o_win)):
                acc = acc + jnp.concatenate(
                    [branch[0][:, cs] * gates_t[r * 3 + c:r * 3 + c + 1, ts],
                     branch[1][:, cs] * gates_t[(B_REP + r) * 3 + c:(B_REP + r) * 3 + c + 1, ts]], axis=0)
            z = u_s[ts, NSA_Z + r * LANES:NSA_Z + (r + 1) * LANES]
            o_ref[ts, r * LANES:(r + 1) * LANES] = (acc.T * _silu(z)).astype(o_ref.dtype)


def _merge_kernel(x_ref, oa_ref, ob_ref, oc_ref, od_ref, g_ref, wmg_ref, mb_ref, wbr_ref, wout_ref, o_ref):
    x = x_ref[...]
    xn = _rms_rows(x, g_ref[...]).astype(BF16)
    merged = jnp.zeros((TS, D_MODEL), F32)
    for br, o_k in enumerate((oa_ref, ob_ref, oc_ref, od_ref)):
        cols = slice(br * D_MODEL, (br + 1) * D_MODEL)
        gate = _sigmoid(jnp.dot(xn, wmg_ref[:, cols], preferred_element_type=F32) + mb_ref[:, cols])
        merged = merged + gate * jnp.dot(o_k[...], wbr_ref[br], preferred_element_type=F32)
    o_ref[...] = x + jnp.dot(merged.astype(BF16), wout_ref[...], preferred_element_type=F32)


def _const_spec(shape):
    nd = len(shape)
    return pl.BlockSpec(shape, lambda b, t: (0,) * nd, pipeline_mode=pl.Buffered(1))


def _tile_spec(width):
    return pl.BlockSpec((None, TS, width), lambda b, t: (b, t, 0))


def _call(kernel, name, x, consts, out_width, out_dtype, scratch, extra_tiles=()):
    batch, seq, _ = x.shape
    in_specs = [_tile_spec(D_MODEL)] + [_tile_spec(a.shape[-1]) for a in extra_tiles]
    in_specs += [_const_spec(c.shape) for c in consts]
    return pl.pallas_call(
        kernel,
        name=name,
        grid=(batch, seq // TS),
        in_specs=in_specs,
        out_specs=_tile_spec(out_width),
        out_shape=jax.ShapeDtypeStruct((batch, seq, out_width), out_dtype),
        scratch_shapes=scratch,
        compiler_params=pltpu.CompilerParams(
            dimension_semantics=("arbitrary", "arbitrary"), vmem_limit_bytes=VMEM_LIMIT),
    )(x, *extra_tiles, *consts)


def _retention_tables():
    pos = np.arange(TS, dtype=np.float64)
    log_g = np.log1p(-np.exp2(-5.0 - np.arange(D_HEADS, dtype=np.float64)))
    rel = pos[:, None] - pos[None, :]
    same = (pos[:, None] // CHUNK) == (pos[None, :] // CHUNK)
    dmat = np.where((rel >= 0) & same, np.exp(log_g[:, None, None] * np.maximum(rel, 0.0)), 0.0)
    dmat = dmat[:, :INTRA, :INTRA]
    inpos = pos % CHUNK
    kdec = np.repeat(np.exp(log_g[None, :] * (CHUNK - 1.0 - inpos)[:, None]), D_QK, axis=1)
    qdec = np.repeat(np.exp(log_g[None, :] * (inpos + 1.0)[:, None]), D_V, axis=1)
    cdec = np.repeat(np.exp(log_g * CHUNK)[None, :], D_V, axis=1)
    return tuple(jnp.asarray(a, F32) for a in (dmat, kdec, qdec, cdec))


def _bf16_pieces(v, n):
    out = []
    rem = np.asarray(v, np.float64)
    for _ in range(n):
        piece = rem.astype(BF16).astype(np.float64)
        out.append(piece)
        rem = rem - piece
    return out


def _position_tile(pos, block_ids=None):
    pos = np.maximum(np.asarray(pos), 0)
    tab = np.zeros((pos.shape[0], LANES), np.float32)
    for i in range(3):
        tab[:, 2 * i] = (pos // 256) * 256
        tab[:, 2 * i + 1] = pos % 256
    tab[:, list(T_ROWS)] = 1.0
    if block_ids is not None:
        tab[np.arange(pos.shape[0]), POS_ROWS + np.asarray(block_ids)] = 1.0
    return jnp.asarray(tab, BF16)


def _nsa_tables(seq):
    n_cmp = seq // CMP_STRIDE
    n_sel = seq // SEL_LEN
    n = np.arange(n_cmp) - 1
    cmp_start = n * CMP_STRIDE
    sel_start = np.arange(n_sel) * SEL_LEN
    ov = ((cmp_start[None, :] < sel_start[:, None] + SEL_LEN) & (cmp_start[None, :] + CMP_LEN > sel_start[:, None])
          & (n[None, :] >= 0) & (n[None, :] <= (seq - CMP_LEN) // CMP_STRIDE))
    slope = np.zeros((B_GROUPS, POS_ROWS, TS // WIN_Q, B_REP, WIN_Q), np.float32)
    for g in range(B_GROUPS):
        for r in range(B_REP):
            pieces = _bf16_pieces(B_SLOPES[g * B_REP + r] * LOG2E, 3)
            for i, piece in enumerate(pieces):
                slope[g, 2 * i:2 * i + 2, :, r, :] = piece
    slope = slope.reshape(B_GROUPS, POS_ROWS, B_REP * TS)
    slopef = np.zeros((B_GROUPS, 1, TS // WIN_Q, B_REP, WIN_Q), np.float32)
    for g in range(B_GROUPS):
        for r in range(B_REP):
            slopef[g, 0, :, r, :] = B_SLOPES[g * B_REP + r] * LOG2E
    slopef = slopef.reshape(B_GROUPS, 1, B_REP * TS)
    tpos = np.arange(seq)
    possel = _position_tile(tpos, tpos // SEL_LEN)
    poswin = _position_tile(np.arange(seq + WINDOW) - WINDOW)
    poscmp = _position_tile(np.arange(n_cmp) * CMP_STRIDE + (CMP_STRIDE - 1))
    return (jnp.asarray(ov, F32), jnp.asarray(slope, F32), jnp.asarray(slopef), possel, poswin, poscmp)


def _perm_heads(w):
    lead = w.shape[:-1]
    return w.reshape(lead + (B_HEADS, B_HEAD_DIM))[..., B_HEAD_PERM, :].reshape(lead + (WIDTH,))


def _cmp_weights(w):
    w = w.reshape(2, CMP_PHASES, B_HEAD_DIM, B_HEAD_DIM)
    eye = jnp.eye(B_GROUPS, dtype=w.dtype)
    big = jnp.einsum('spde,gh->spgdhe', w, eye).reshape(2, CMP_PHASES * B_KV, B_KV)
    return jnp.concatenate([big[0], big[1]], axis=1).astype(BF16)


def _cmp_pos(pos):
    p = jnp.tile(pos.reshape(2, CMP_PHASES, 1, B_HEAD_DIM), (1, 1, B_GROUPS, 1)).reshape(2, CMP_PHASES * B_KV)
    return jnp.concatenate([p, jnp.zeros((14, CMP_PHASES * B_KV), p.dtype)], axis=0)


def _block_diag(w):
    eye = jnp.eye(C_BLOCKS, dtype=w.dtype)
    return jnp.einsum('ncd,nm->ncmd', w, eye).reshape(WIDTH, WIDTH).astype(BF16)


def kernel(x, norm_g, w_in, lb_logits, a_norm_g, b_q_norm_g, b_k_norm_g, b_cmp_pos, b_cmp_wk, b_cmp_wv,
           c_conv_w, c_conv_b, c_w_ra, c_b_ra, c_w_ri, c_b_ri, c_lambda, d_norm_g, merge_b, w_branch, w_out):
    batch, seq, _ = x.shape
    depth = norm_g.shape[0]
    assert seq % TS == 0 and x.shape[-1] == D_MODEL

    p_lb = jax.nn.softmax(lb_logits.astype(F32), axis=0)
    lower_bounds = jnp.cumsum(p_lb, axis=0) - p_lb[0:1]
    dmat, kdec, qdec, cdec = _retention_tables()
    nsa_tabs = _nsa_tables(seq)
    row = lambda v: v.reshape(1, -1).astype(F32)
    vm = lambda shape, dt=F32: pltpu.VMEM(shape, dt)
    n_cmp = seq // CMP_STRIDE

    for l in range(depth):
        w = w_in[l]
        g = row(norm_g[l])
        w_a = w[:, 0:2048].astype(BF16)
        b0 = 2048
        gate_cols = jnp.pad(w[:, b0 + 1280:b0 + 1304], ((0, 0), (0, LANES - 3 * B_HEADS)))
        w_b = jnp.concatenate([_perm_heads(w[:, b0:b0 + 512]), w[:, b0 + 512:b0 + 1280], gate_cols,
                               _perm_heads(w[:, b0 + 1304:b0 + 1816])], axis=1).astype(BF16)
        c0 = b0 + 1816
        w_c = w[:, c0:c0 + 1024].astype(BF16)
        d0 = c0 + 1024
        w_d = w[:, d0:d0 + 1536].astype(BF16)
        m0 = d0 + 1536
        w_m = w[:, m0:m0 + N_BRANCH * D_MODEL].astype(BF16)

        o_a = _call(_hgrn_kernel, "hgrn", x, [g, w_a, row(lower_bounds[l]), row(a_norm_g[l])], WIDTH, BF16,
                    [vm((A_HEADS, A_HEAD_DIM, A_HEAD_DIM))])

        nsa_consts = [g, w_b, row(jnp.tile(b_q_norm_g[l], B_HEADS)), row(jnp.tile(b_k_norm_g[l], B_GROUPS)),
                      _cmp_pos(b_cmp_pos[l]), _cmp_weights(b_cmp_wk[l]), _cmp_weights(b_cmp_wv[l]), *nsa_tabs]
        nsa_scratch = [vm((TS, NSA_COLS)), vm((seq, B_KV), BF16), vm((seq // TS, B_KV, TS), BF16),
                       vm((seq + WINDOW, B_KV), BF16), vm(((seq + WINDOW) // Q_BLOCK, B_KV, Q_BLOCK), BF16),
                       vm((n_cmp, B_KV), BF16), vm((n_cmp, B_KV), BF16),
                       vm((TS // CMP_STRIDE, B_KV)), vm((TS // CMP_STRIDE, B_KV)), vm((2, TS, B_KV))]
        nsa = lambda online: _call(functools.partial(_nsa_kernel, seq, online), "nsa_online" if online else "nsa",
                                   x, nsa_consts, WIDTH, BF16, nsa_scratch)
        score_bound = (B_HEAD_DIM * B_HEAD_DIM ** -0.5 * LOG2E * jnp.max(jnp.abs(b_q_norm_g[l]))
                       * jnp.max(jnp.abs(b_k_norm_g[l])))
        o_b = lax.cond(score_bound <= FIXED_SHIFT_RANGE, lambda: nsa(False), lambda: nsa(True))

        o_c = _call(_rglru_kernel, "rglru", x,
                    [g, w_c, c_conv_w[l].astype(F32), row(c_conv_b[l]), _block_diag(c_w_ra[l]), row(c_b_ra[l]),
                     _block_diag(c_w_ri[l]), row(c_b_ri[l]), row(c_lambda[l])], WIDTH, BF16,
                    [vm((TS + 2 * CONV_PAD, WIDTH)), vm((8, WIDTH))] + [vm((WIDTH // LANES, TS, LANES))] * 4)

        o_d = _call(_ret_kernel, "retention", x, [g, w_d, dmat, kdec, qdec, cdec, row(d_norm_g[l])], WIDTH, BF16,
                    [vm((D_HEADS, LANES, D_V))])

        wbr = w_branch[l]
        wbr = jnp.stack([wbr[0], _perm_heads(wbr[1].T).T, wbr[2], wbr[3]], axis=0).astype(BF16)
        x = _call(_merge_kernel, "merge", x, [g, w_m, row(merge_b[l]), wbr, w_out[l].astype(BF16)],
                  D_MODEL, x.dtype, [], extra_tiles=(o_a, o_b, o_c, o_d))
    return x
```

```python
import functools

import numpy as np
import jax
import jax.numpy as jnp
from jax import lax
from jax.experimental import pallas as pl
from jax.experimental.pallas import tpu as pltpu

F32 = jnp.float32
BF16 = jnp.bfloat16

D_MODEL = 1024
WIDTH = 512
EPS = 1e-6
NEG_INF = -1e30
BIG = 1e30

A_HEADS = 4
A_HEAD_DIM = 128
CHUNK = 64

B_HEADS = 8
B_GROUPS = 2
B_REP = B_HEADS // B_GROUPS
B_HEAD_DIM = 64
B_KV = B_GROUPS * B_HEAD_DIM
CMP_LEN = 32
CMP_STRIDE = 16
SEL_LEN = 64
SEL_TOPN = 8
WINDOW = 256
Q_BLOCK = 128

C_BLOCKS = 8
C_BLOCK_DIM = 64
CONV_WIDTH = 4
RG_C = 8.0

D_HEADS = 4
D_QK = 64
D_V = 128

N_BRANCH = 4

TS = 512
LANES = 128
VMEM_LIMIT = 56 * 1024 * 1024
LOG2E = 1.4426950408889634

B_HEAD_PERM = (0, 4, 1, 5, 2, 6, 3, 7)
B_SLOPES = tuple(float(2.0 ** (-8.0 * (h + 1) / B_HEADS)) for h in range(B_HEADS))


def _dot(a, b):
    return jnp.dot(a.astype(BF16), b.astype(BF16), preferred_element_type=F32)


def _dot_nt(a, b):
    return lax.dot_general(a.astype(BF16), b.astype(BF16), (((1,), (1,)), ((), ())),
                           preferred_element_type=F32)


def _dot_tn(a, b):
    return lax.dot_general(a.astype(BF16), b.astype(BF16), (((0,), (0,)), ((), ())),
                           preferred_element_type=F32)


def _rms_rows(x, g):
    ms = jnp.mean(x * x, axis=-1, keepdims=True)
    return x * lax.rsqrt(ms + EPS) * g


def _sigmoid(x):
    return 0.5 * jnp.tanh(0.5 * x) + 0.5


def _silu(z):
    return z * _sigmoid(z)


def _iota(shape, dim):
    return lax.broadcasted_iota(jnp.int32, shape, dim)


def _normed_input(x_ref, g_ref):
    return _rms_rows(x_ref[...], g_ref[...]).astype(BF16)


N_CHUNKS = TS // CHUNK
INTRA = 256


def _chunk_broadcast(v, row):
    v3 = v.reshape(N_CHUNKS, CHUNK, v.shape[-1])
    return jnp.broadcast_to(v3[:, row:row + 1, :], v3.shape).reshape(v.shape)


def _chunk_rows(v, row):
    return v.reshape(N_CHUNKS, CHUNK, v.shape[-1])[:, row, :]


def _chunk_block_diag(v):
    n = v.shape[0] // CHUNK
    zero = jnp.zeros((CHUNK, v.shape[-1]), v.dtype)
    return jnp.concatenate(
        [jnp.concatenate([zero] * c + [v[c * CHUNK:(c + 1) * CHUNK]] + [zero] * (n - 1 - c), axis=1)
         for c in range(n)], axis=0)


def _chunk_diag_blocks(v, width):
    n = v.shape[0] // CHUNK
    return jnp.concatenate([v[c * CHUNK:(c + 1) * CHUNK, c * width:(c + 1) * width] for c in range(n)], axis=0)


def _intra_rows(i):
    return slice(i * INTRA, (i + 1) * INTRA)


def _hgrn_kernel(x_ref, g_ref, w_ref, lb_ref, ng_ref, o_ref, st_s):
    @pl.when(pl.program_id(1) == 0)
    def _():
        st_s[...] = jnp.zeros_like(st_s)

    u = jnp.dot(_normed_input(x_ref, g_ref), w_ref[...], preferred_element_type=F32)
    q = u[:, 0:WIDTH]
    fl = u[:, WIDTH:2 * WIDTH]
    lb = lb_ref[...]
    f = lb + (1.0 - lb) * jax.nn.sigmoid(fl)
    k = (1.0 - lb) * jax.nn.sigmoid(-fl)
    vb = u[:, 2 * WIDTH:3 * WIDTH].astype(BF16)
    b = jnp.log(f)
    row_in_chunk = _iota((TS, WIDTH), 0) & (CHUNK - 1)
    s = 1
    while s < CHUNK:
        b = b + jnp.where(row_in_chunk >= s, pltpu.roll(b, s, 0), 0.0)
        s *= 2
    b_mid = _chunk_broadcast(b, CHUNK // 2 - 1)
    b_end = _chunk_broadcast(b, CHUNK - 1)
    qe = (q * jnp.exp(b - b_mid)).astype(BF16)
    ke = (k * jnp.exp(b_mid - b)).astype(BF16)
    k_end = (k * jnp.exp(b_end - b)).astype(BF16)
    qb = (q * jnp.exp(b)).astype(BF16)
    dec = jnp.exp(_chunk_rows(b, CHUNK - 1))

    ri, ci = _iota((INTRA, INTRA), 0), _iota((INTRA, INTRA), 1)
    block_causal = jnp.logical_and(ri >= ci, (ri >> 6) == (ci >> 6))
    ng = ng_ref[...]
    for h in range(A_HEADS):
        sl = slice(h * A_HEAD_DIM, (h + 1) * A_HEAD_DIM)
        intra = []
        for i in range(TS // INTRA):
            rows = slice(i * INTRA, (i + 1) * INTRA)
            sc = jnp.where(block_causal, _dot_nt(qe[rows, sl], ke[rows, sl]), 0.0)
            intra.append(_dot(sc, vb[rows, sl]))
        kv_all = jnp.concatenate([_dot_tn(vb[_intra_rows(i), sl], _chunk_block_diag(k_end[_intra_rows(i), sl]))
                                  for i in range(TS // INTRA)], axis=1)
        st = st_s[h]
        states = []
        for c in range(N_CHUNKS):
            states.append(st.astype(BF16))
            st = st * dec[c:c + 1, sl] + kv_all[:, c * A_HEAD_DIM:(c + 1) * A_HEAD_DIM]
        st_s[h] = st
        per = INTRA // CHUNK
        inter = jnp.concatenate(
            [_chunk_diag_blocks(_dot_nt(qb[_intra_rows(i), sl], jnp.concatenate(states[i * per:(i + 1) * per], axis=0)),
                                A_HEAD_DIM) for i in range(TS // INTRA)], axis=0)
        o = jnp.concatenate(intra, axis=0) + inter
        z = u[:, 3 * WIDTH + h * A_HEAD_DIM:3 * WIDTH + (h + 1) * A_HEAD_DIM]
        o_ref[:, sl] = (_rms_rows(o, ng[:, sl]) * _silu(z)).astype(o_ref.dtype)


def _ret_kernel(x_ref, g_ref, w_ref, dmat_ref, kdec_ref, qdec_ref, cdec_ref, ng_ref, o_ref, st_s):
    @pl.when(pl.program_id(1) == 0)
    def _():
        st_s[...] = jnp.zeros_like(st_s)

    u = jnp.dot(_normed_input(x_ref, g_ref), w_ref[...], preferred_element_type=F32)
    nqk = D_HEADS * D_QK
    q = u[:, 0:nqk] * (D_QK ** -0.5)
    k = u[:, nqk:2 * nqk]
    kb = k.astype(BF16)
    kd = (k * kdec_ref[...]).astype(BF16)
    vb = u[:, 2 * nqk:2 * nqk + WIDTH].astype(BF16)
    first_half = _iota((TS, LANES), 1) < D_QK
    qdec = qdec_ref[...]
    cdec = cdec_ref[...]
    ng = ng_ref[...]
    for h in range(D_HEADS):
        pair = slice((h // 2) * LANES, (h // 2 + 1) * LANES)
        vsl = slice(h * D_V, (h + 1) * D_V)
        qm = jnp.where(first_half if h % 2 == 0 else jnp.logical_not(first_half), q[:, pair], 0.0).astype(BF16)
        intra = []
        for i in range(TS // INTRA):
            rows = slice(i * INTRA, (i + 1) * INTRA)
            sc = _dot_nt(qm[rows], kb[rows, pair]) * dmat_ref[h]
            intra.append(_dot(sc, vb[rows, vsl]))
        kv_all = jnp.concatenate([_dot_tn(kd[_intra_rows(i), pair], _chunk_block_diag(vb[_intra_rows(i), vsl]))
                                  for i in range(TS // INTRA)], axis=1)
        st = st_s[h]
        states = []
        for c in range(N_CHUNKS):
            states.append(st.astype(BF16))
            st = st * cdec[:, vsl] + kv_all[:, c * D_V:(c + 1) * D_V]
        st_s[h] = st
        per = INTRA // CHUNK
        inter = jnp.concatenate(
            [_chunk_diag_blocks(_dot(qm[_intra_rows(i)], jnp.concatenate(states[i * per:(i + 1) * per], axis=1)), D_V)
             for i in range(TS // INTRA)], axis=0)
        o = jnp.concatenate(intra, axis=0) + inter * qdec[:, vsl]
        oc = o - jnp.mean(o, axis=-1, keepdims=True)
        var = jnp.mean(oc * oc, axis=-1, keepdims=True)
        z = u[:, 2 * nqk + WIDTH + h * D_V:2 * nqk + WIDTH + (h + 1) * D_V]
        o_ref[:, vsl] = (oc * lax.rsqrt(var + EPS) * ng[:, vsl] * _silu(z)).astype(o_ref.dtype)


CONV_PAD = 8
GROUP = 8


def _scan_steps(a, b, row, length):
    s = 1
    while s < length:
        keep = row >= s
        a_prev = jnp.where(keep, pltpu.roll(a, s, 0), 1.0)
        b_prev = jnp.where(keep, pltpu.roll(b, s, 0), 0.0)
        b = a * b_prev + b
        a = a * a_prev
        s *= 2
    return a, b


def _rglru_kernel(x_ref, g_ref, w_ref, cw_ref, cb_ref, wra_ref, bra_ref, wri_ref, bri_ref, lam_ref,
                  o_ref, ext_s, h_s, a_s, b_s, ca_s, cb_s):
    t = pl.program_id(1)

    @pl.when(t == 0)
    def _():
        ext_s[0:CONV_PAD, :] = jnp.zeros((CONV_PAD, WIDTH), F32)
        h_s[...] = jnp.zeros_like(h_s)

    @pl.when(t > 0)
    def _():
        ext_s[0:CONV_PAD, :] = ext_s[TS:TS + CONV_PAD, :]

    u = jnp.dot(_normed_input(x_ref, g_ref), w_ref[...], preferred_element_type=F32)
    ext_s[CONV_PAD:CONV_PAD + TS, :] = u[:, 0:WIDTH]
    cz = u[:, WIDTH:2 * WIDTH]
    cw = cw_ref[...]
    xc = cb_ref[...] + cw[CONV_WIDTH - 1:CONV_WIDTH, :] * u[:, 0:WIDTH]
    for j in range(CONV_WIDTH - 1):
        back = CONV_WIDTH - 1 - j
        xc = xc + cw[j:j + 1, :] * ext_s[CONV_PAD - back:CONV_PAD - back + TS, :]
    r = _sigmoid(_dot(xc, wra_ref[...]) + bra_ref[...])
    ig = _sigmoid(_dot(xc, wri_ref[...]) + bri_ref[...])
    neg_lam = -lam_ref[...]
    softplus = jnp.maximum(neg_lam, 0.0) + jnp.log1p(jnp.exp(-jnp.abs(neg_lam)))
    log_a = -RG_C * r * softplus
    a = jnp.exp(log_a)
    b = jnp.sqrt(1.0 - a * a) * (ig * xc)
    n_groups = TS // GROUP
    planes = WIDTH // LANES
    rows_of = lambda r: pl.ds(r, n_groups, stride=GROUP)
    for j in range(planes):
        a_s[j] = a[:, j * LANES:(j + 1) * LANES]
        b_s[j] = b[:, j * LANES:(j + 1) * LANES]
    cum_a = [a_s[j, rows_of(0), :] for j in range(planes)]
    cum_b = [b_s[j, rows_of(0), :] for j in range(planes)]
    for r in range(GROUP):
        for j in range(planes):
            if r > 0:
                a_r = a_s[j, rows_of(r), :]
                cum_b[j] = a_r * cum_b[j] + b_s[j, rows_of(r), :]
                cum_a[j] = a_r * cum_a[j]
            ca_s[j, rows_of(r), :] = cum_a[j]
            cb_s[j, rows_of(r), :] = cum_b[j]
    group = _iota((n_groups, WIDTH), 0)
    a_tot, b_tot = _scan_steps(jnp.concatenate(cum_a, axis=1), jnp.concatenate(cum_b, axis=1), group, n_groups)
    h0 = h_s[0:1, :]
    h_after = a_tot * h0 + b_tot
    h_before = jnp.where(group == 0, h0, pltpu.roll(h_after, 1, 0))
    h_s[0:1, :] = h_after[n_groups - 1:n_groups, :]
    for r in range(GROUP):
        for j in range(planes):
            lanes = slice(j * LANES, (j + 1) * LANES)
            a_s[j, rows_of(r), :] = ca_s[j, rows_of(r), :] * h_before[:, lanes] + cb_s[j, rows_of(r), :]
    h = jnp.concatenate([a_s[j] for j in range(planes)], axis=1)
    o_ref[...] = (h * _silu(cz)).astype(o_ref.dtype)


NSA_Q = 0
NSA_KC = 512
NSA_VC = 640
NSA_KS = 768
NSA_VS = 896
NSA_KW = 1024
NSA_VW = 1152
NSA_G = 1280
NSA_Z = 1408
NSA_COLS = 1920
SEL_KEYS = TS
WIN_Q = 256
WIN_KEYS = Q_BLOCK + WINDOW
CMP_PHASES = CMP_STRIDE
AUG = 2 * LANES
POS_ROWS = 16
MASK_BIG = float(2.0 ** 40)
SUM_ROWS = 16
SEL_COLS = 256
SEL_AHEAD = 3
T_ROWS = (6, 7, 8)
FIXED_SHIFT_RANGE = 60.0


def _half_rms(v, first_half):
    v2 = v * v
    s0 = jnp.sum(jnp.where(first_half, v2, 0.0), axis=-1, keepdims=True)
    s1 = jnp.sum(jnp.where(first_half, 0.0, v2), axis=-1, keepdims=True)
    ms = jnp.where(first_half, s0, s1) * (1.0 / B_HEAD_DIM)
    return v * lax.rsqrt(ms + EPS)


def _rank_select(imp, n_live):
    n_sel, n_query = imp.shape
    cnt = []
    for v0 in range(0, n_live, 8):
        mine, jmine = imp[v0:v0 + 8], v0 + _iota((8, n_query), 0)
        c = jnp.zeros((8, n_query), F32)
        for jp in range(n_live):
            other = imp[jp:jp + 1, :]
            if jp < v0:
                ahead = other >= mine
            elif jp >= v0 + 8:
                ahead = other > mine
            else:
                ahead = jnp.where(jmine > jp, jnp.where(other >= mine, 1.0, 0.0),
                                  jnp.where(other > mine, 1.0, 0.0)) > 0.5
            c = c + jnp.where(ahead, 1.0, 0.0)
        cnt.append(jnp.where(c < SEL_TOPN, 0.0, -MASK_BIG))
    if n_live < n_sel:
        cnt.append(jnp.full((n_sel - n_live, n_query), -MASK_BIG, F32))
    return jnp.concatenate(cnt, axis=0).astype(BF16)


def _softmax_cols(s):
    p = jnp.exp2(s - jnp.max(s, axis=0, keepdims=True))
    return p, jnp.sum(p, axis=0, keepdims=True)


def _nsa_kernel(seq, online, x_ref, g_ref, w_ref, qg_ref, kg_ref, pos12_ref, w12k_ref, w12v_ref,
                ov_ref, slope_ref, slopef_ref, possel_ref, poswin_ref, poscmp_ref, o_ref,
                u_s, ks_c, vst_c, kw_c, vwt_c, kc_c, vc_c, p1k_s, p1v_s, kv_s):
    n_cmp = seq // CMP_STRIDE
    n_sel = seq // SEL_LEN
    ti = pl.program_id(1)

    @pl.when(ti == 0)
    def _():
        for ref in (ks_c, vst_c, kw_c, vwt_c, kc_c, vc_c, p1k_s, p1v_s):
            ref[...] = jnp.zeros_like(ref)

    u_s[...] = jnp.dot(_normed_input(x_ref, g_ref), w_ref[...], preferred_element_type=F32)
    row0 = pl.multiple_of(ti * TS, TS)
    kg = kg_ref[...]
    half_t = _iota((TS, LANES), 1) < B_HEAD_DIM

    ks_c[pl.ds(row0, TS), :] = (_half_rms(u_s[:, NSA_KS:NSA_KS + B_KV], half_t) * kg).astype(BF16)
    vst_c[ti] = u_s[:, NSA_VS:NSA_VS + B_KV].T.astype(BF16)
    kw_c[pl.ds(row0 + WINDOW, TS), :] = (_half_rms(u_s[:, NSA_KW:NSA_KW + B_KV], half_t) * kg).astype(BF16)
    for i in range(TS // Q_BLOCK):
        vwt_c[ti * (TS // Q_BLOCK) + WINDOW // Q_BLOCK + i] = (
            u_s[i * Q_BLOCK:(i + 1) * Q_BLOCK, NSA_VW:NSA_VW + B_KV].T.astype(BF16))

    n_grp = TS // CMP_STRIDE
    first_row = _iota((n_grp, B_KV), 0) == 0
    half_g = _iota((n_grp, LANES), 1) < B_HEAD_DIM
    crow = pl.ds(pl.multiple_of(ti * n_grp, n_grp), n_grp)
    for i, (col, w12_ref, p1_s, cache, is_key) in enumerate(((NSA_KC, w12k_ref, p1k_s, kc_c, True),
                                                             (NSA_VC, w12v_ref, p1v_s, vc_c, False))):
        kv_s[i] = u_s[:, col:col + B_KV]
        grouped = jnp.concatenate([kv_s[i, pl.ds(r, n_grp, stride=CMP_STRIDE), :] for r in range(CMP_PHASES)],
                                  axis=1)
        both = _dot(jnp.concatenate([grouped, pos12_ref[...]], axis=0), w12_ref[...])
        p1 = both[0:n_grp, 0:B_KV] + both[n_grp:n_grp + 1, 0:B_KV]
        p2 = both[0:n_grp, B_KV:2 * B_KV] + both[n_grp + 1:n_grp + 2, B_KV:2 * B_KV]
        p1_prev = jnp.where(first_row, p1_s[n_grp - 1:n_grp, :], pltpu.roll(p1, 1, 0))
        p1_s[...] = p1
        blk = p1_prev + p2
        if is_key:
            blk = _half_rms(blk, half_g) * kg
        cache[crow, :] = blk.astype(BF16)

    qg = qg_ref[...]
    n_q = B_REP * TS
    halves = TS // WIN_Q

    def cols(a):
        return jnp.concatenate([a[:, hb * WIN_Q:(hb + 1) * WIN_Q] for hb in range(halves) for _ in range(B_REP)],
                               axis=1)

    def col_block(hb, r):
        return slice((hb * B_REP + r) * WIN_Q, (hb * B_REP + r + 1) * WIN_Q)

    q_n = [_half_rms(u_s[:, NSA_Q + r * LANES:NSA_Q + (r + 1) * LANES], half_t)
           * qg[:, r * LANES:(r + 1) * LANES] * (B_HEAD_DIM ** -0.5 * LOG2E) for r in range(B_REP)]
    q_t = jnp.concatenate([q_n[r][hb * WIN_Q:(hb + 1) * WIN_Q].T for hb in range(halves) for r in range(B_REP)],
                          axis=1)
    q_rows_first = _iota((LANES, n_q), 0) < B_HEAD_DIM
    t_col = (row0 + cols(_iota((1, TS), 1))).astype(F32)
    prow = _iota((POS_ROWS, n_q), 0)
    slope_rows = []
    for g in range(B_GROUPS):
        rest = -slopef_ref[g] * t_col
        block = slope_ref[g]
        for row in T_ROWS:
            piece = rest.astype(BF16).astype(F32)
            block = block + jnp.where(prow == row, piece, 0.0)
            rest = rest - piece
        slope_rows.append(block.astype(BF16))
    base = [jnp.concatenate([jnp.where(q_rows_first, q_t, 0.0).astype(BF16), slope_rows[0]], axis=0),
            jnp.concatenate([jnp.where(q_rows_first, 0.0, q_t).astype(BF16), slope_rows[1]], axis=0)]
    n_base = LANES + POS_ROWS
    groups = range(B_GROUPS)

    t_c = row0 + _iota((n_cmp, TS), 1)
    nprime = _iota((n_cmp, TS), 0)
    madd_c = cols(jnp.where(t_c >= nprime * CMP_STRIDE + (CMP_STRIDE - 1),
                            jnp.where(nprime >= 1, 0.0, NEG_INF), NEG_INF))
    any_c = cols(jnp.where(row0 + _iota((1, TS), 1) >= CMP_LEN - 1, 1.0, 0.0))
    kaug_c = jnp.concatenate([kc_c[...], poscmp_ref[...]], axis=1)
    vct = vc_c[...].astype(F32).T.astype(BF16)
    zeros_c = jnp.zeros((AUG - n_base, n_q), BF16)
    s_c = [jnp.dot(kaug_c, jnp.concatenate([base[g], zeros_c], axis=0), preferred_element_type=F32) + madd_c
           for g in groups]
    p_c = []
    for g in groups:
        p, l = _softmax_cols(s_c[g])
        p_c.append(p * (any_c / l))
    o_cmp = [_dot(vct[g * B_HEAD_DIM:(g + 1) * B_HEAD_DIM], p_c[g]) for g in groups]
    blk_t = (row0 + _iota((n_sel, TS), 1)) >> 6
    jrow = _iota((n_sel, TS), 0)
    q_aug = []
    for g in groups:
        p_sum = jnp.concatenate(
            [sum(p_c[g][:, col_block(hb, r)] for r in range(B_REP)) for hb in range(halves)], axis=1)
        imp = jnp.dot(ov_ref[...], p_sum, precision=lax.Precision.HIGHEST,
                      preferred_element_type=F32)
        imp = jnp.where(jrow == blk_t, BIG, jnp.where(jrow < blk_t, imp, -BIG))
        sel_neg = lax.switch(ti, [functools.partial(_rank_select, n_live=(k + 1) * (TS // SEL_LEN))
                                  for k in range(seq // TS)], imp)
        q_aug.append(jnp.concatenate([base[g], cols(sel_neg),
                                      jnp.zeros((AUG - n_base - n_sel, n_q), BF16)], axis=0))

    def sel_step(kc, st, diagonal):
        k0 = pl.multiple_of(kc * SEL_KEYS, SEL_KEYS)
        kaug = jnp.concatenate([ks_c[pl.ds(k0, SEL_KEYS), :], possel_ref[pl.ds(k0, SEL_KEYS), :]], axis=1)
        ones = jnp.ones((SUM_ROWS, SEL_KEYS), BF16)
        v_aug = [jnp.concatenate([vst_c[kc, g * B_HEAD_DIM:(g + 1) * B_HEAD_DIM, :], ones], axis=0) for g in groups]
        if diagonal:
            madd = cols(jnp.where(_iota((SEL_KEYS, TS), 0) <= _iota((SEL_KEYS, TS), 1), 0.0, NEG_INF))
        units = [(g, slice(b * SEL_COLS, (b + 1) * SEL_COLS)) for b in range(n_q // SEL_COLS) for g in groups]

        def n_keys(cs):
            return (cs.start // (B_REP * WIN_Q) + 1) * WIN_Q if diagonal else SEL_KEYS

        def qk(g, cs):
            n = n_keys(cs)
            s = jnp.dot(kaug[0:n], q_aug[g][:, cs], preferred_element_type=F32)
            return s + madd[0:n, cs] if diagonal else s

        scores = [qk(*u) for u in units[:SEL_AHEAD]]
        m_out, acc_out = [[] for _ in groups], [[] for _ in groups]
        for i, (g, cs) in enumerate(units):
            if i + SEL_AHEAD < len(units):
                scores.append(qk(*units[i + SEL_AHEAD]))
            s = scores[i]
            m_old, acc_old = st[2 * g][:, cs], st[2 * g + 1][:, cs]
            if online:
                m_new = jnp.maximum(m_old, jnp.max(s, axis=0, keepdims=True))
                s = s - m_new
                acc_old = jnp.exp2(m_old - m_new) * acc_old
            else:
                m_new = m_old
            p = jnp.exp2(s.astype(BF16))
            m_out[g].append(m_new)
            acc_out[g].append(acc_old + jnp.dot(v_aug[g][:, 0:n_keys(cs)], p, preferred_element_type=F32))
        out = []
        for g in groups:
            out += [jnp.concatenate(m_out[g], axis=1), jnp.concatenate(acc_out[g], axis=1)]
        return tuple(out)

    init = []
    for g in groups:
        init += [jnp.full((1, n_q), NEG_INF, F32), jnp.zeros((B_HEAD_DIM + SUM_ROWS, n_q), F32)]
    st = lax.fori_loop(0, ti, functools.partial(sel_step, diagonal=False), tuple(init))
    st = sel_step(ti, st, diagonal=True)
    o_sel = [st[2 * g + 1][0:B_HEAD_DIM] / st[2 * g + 1][B_HEAD_DIM:B_HEAD_DIM + 1] for g in groups]

    per_half = WIN_Q // Q_BLOCK
    o_win_pass = []
    for qi in range(TS // Q_BLOCK):
        hb, qq = divmod(qi, per_half)
        w0 = row0 + qi * Q_BLOCK
        kaug_w = jnp.concatenate([kw_c[pl.ds(w0, WIN_KEYS), :], poswin_ref[pl.ds(w0, WIN_KEYS), :]], axis=1)
        wblk = ti * (TS // Q_BLOCK) + qi
        ones = jnp.ones((SUM_ROWS, WIN_KEYS), BF16)
        vwt = [jnp.concatenate(
            [jnp.concatenate([vwt_c[wblk + i, g * B_HEAD_DIM:(g + 1) * B_HEAD_DIM, :]
                              for i in range(WIN_KEYS // Q_BLOCK)], axis=1), ones], axis=0) for g in groups]
        spos = (w0 - WINDOW) + _iota((WIN_KEYS, Q_BLOCK), 0)
        dist_w = (w0 + _iota((WIN_KEYS, Q_BLOCK), 1)) - spos
        madd_w = jnp.where(dist_w >= 0, jnp.where(dist_w < WINDOW, jnp.where(spos >= 0, 0.0, NEG_INF), NEG_INF),
                           NEG_INF)
        madd_w = jnp.concatenate([madd_w] * B_REP, axis=1)
        q_w = [jnp.concatenate([q_aug[g][:, col_block(hb, r).start + qq * Q_BLOCK:
                                         col_block(hb, r).start + (qq + 1) * Q_BLOCK] for r in range(B_REP)], axis=1)
               for g in groups]
        s_w = [jnp.dot(kaug_w, q_w[g], preferred_element_type=F32) + madd_w for g in groups]
        if online:
            s_w = [s - jnp.max(s, axis=0, keepdims=True) for s in s_w]
        p_w = [jnp.exp2(s.astype(BF16)) for s in s_w]
        pv = [jnp.dot(vwt[g], p_w[g], preferred_element_type=F32) for g in groups]
        o_win_pass.append([x[0:B_HEAD_DIM] / x[B_HEAD_DIM:B_HEAD_DIM + 1] for x in pv])
    o_win = [jnp.concatenate([o_win_pass[hb * per_half + qq][g][:, r * Q_BLOCK:(r + 1) * Q_BLOCK]
                              for hb in range(halves) for r in range(B_REP) for qq in range(per_half)], axis=1)
             for g in groups]

    gates_t = _sigmoid(u_s[:, NSA_G:NSA_G + LANES]).T
    for hb in range(halves):
        ts = slice(hb * WIN_Q, (hb + 1) * WIN_Q)
        for r in range(B_REP):
            cs = col_block(hb, r)
            acc = jnp.zeros((LANES, WIN_Q), F32)
            for c, branch in enumerate((o_cmp, o_sel, o_win)):
                acc = acc + jnp.concatenate(
                    [branch[0][:, cs] * gates_t[r * 3 + c:r * 3 + c + 1, ts],
                     branch[1][:, cs] * gates_t[(B_REP + r) * 3 + c:(B_REP + r) * 3 + c + 1, ts]], axis=0)
            z = u_s[ts, NSA_Z + r * LANES:NSA_Z + (r + 1) * LANES]
            o_ref[ts, r * LANES:(r + 1) * LANES] = (acc.T * _silu(z)).astype(o_ref.dtype)


def _merge_kernel(x_ref, oa_ref, ob_ref, oc_ref, od_ref, g_ref, wmg_ref, mb_ref, wbr_ref, wout_ref, o_ref):
    x = x_ref[...]
    xn = _rms_rows(x, g_ref[...]).astype(BF16)
    merged = jnp.zeros((TS, D_MODEL), F32)
    for br, o_k in enumerate((oa_ref, ob_ref, oc_ref, od_ref)):
        cols = slice(br * D_MODEL, (br + 1) * D_MODEL)
        gate = _sigmoid(jnp.dot(xn, wmg_ref[:, cols], preferred_element_type=F32) + mb_ref[:, cols])
        merged = merged + gate * jnp.dot(o_k[...], wbr_ref[br], preferred_element_type=F32)
    o_ref[...] = x + jnp.dot(merged.astype(BF16), wout_ref[...], preferred_element_type=F32)


def _const_spec(shape):
    nd = len(shape)
    return pl.BlockSpec(shape, lambda b, t: (0,) * nd, pipeline_mode=pl.Buffered(1))


def _tile_spec(width):
    return pl.BlockSpec((None, TS, width), lambda b, t: (b, t, 0))


def _call(kernel, name, x, consts, out_width, out_dtype, scratch, extra_tiles=()):
    batch, seq, _ = x.shape
    in_specs = [_tile_spec(D_MODEL)] + [_tile_spec(a.shape[-1]) for a in extra_tiles]
    in_specs += [_const_spec(c.shape) for c in consts]
    return pl.pallas_call(
        kernel,
        name=name,
        grid=(batch, seq // TS),
        in_specs=in_specs,
        out_specs=_tile_spec(out_width),
        out_shape=jax.ShapeDtypeStruct((batch, seq, out_width), out_dtype),
        scratch_shapes=scratch,
        compiler_params=pltpu.CompilerParams(
            dimension_semantics=("arbitrary", "arbitrary"), vmem_limit_bytes=VMEM_LIMIT),
    )(x, *extra_tiles, *consts)


def _retention_tables():
    pos = np.arange(TS, dtype=np.float64)
    log_g = np.log1p(-np.exp2(-5.0 - np.arange(D_HEADS, dtype=np.float64)))
    rel = pos[:, None] - pos[None, :]
    same = (pos[:, None] // CHUNK) == (pos[None, :] // CHUNK)
    dmat = np.where((rel >= 0) & same, np.exp(log_g[:, None, None] * np.maximum(rel, 0.0)), 0.0)
    dmat = dmat[:, :INTRA, :INTRA]
    inpos = pos % CHUNK
    kdec = np.repeat(np.exp(log_g[None, :] * (CHUNK - 1.0 - inpos)[:, None]), D_QK, axis=1)
    qdec = np.repeat(np.exp(log_g[None, :] * (inpos + 1.0)[:, None]), D_V, axis=1)
    cdec = np.repeat(np.exp(log_g * CHUNK)[None, :], D_V, axis=1)
    return tuple(jnp.asarray(a, F32) for a in (dmat, kdec, qdec, cdec))


def _bf16_pieces(v, n):
    out = []
    rem = np.asarray(v, np.float64)
    for _ in range(n):
        piece = rem.astype(BF16).astype(np.float64)
        out.append(piece)
        rem = rem - piece
    return out


def _position_tile(pos, block_ids=None):
    pos = np.maximum(np.asarray(pos), 0)
    tab = np.zeros((pos.shape[0], LANES), np.float32)
    for i in range(3):
        tab[:, 2 * i] = (pos // 256) * 256
        tab[:, 2 * i + 1] = pos % 256
    tab[:, list(T_ROWS)] = 1.0
    if block_ids is not None:
        tab[np.arange(pos.shape[0]), POS_ROWS + np.asarray(block_ids)] = 1.0
    return jnp.asarray(tab, BF16)


def _nsa_tables(seq):
    n_cmp = seq // CMP_STRIDE
    n_sel = seq // SEL_LEN
    n = np.arange(n_cmp) - 1
    cmp_start = n * CMP_STRIDE
    sel_start = np.arange(n_sel) * SEL_LEN
    ov = ((cmp_start[None, :] < sel_start[:, None] + SEL_LEN) & (cmp_start[None, :] + CMP_LEN > sel_start[:, None])
          & (n[None, :] >= 0) & (n[None, :] <= (seq - CMP_LEN) // CMP_STRIDE))
    slope = np.zeros((B_GROUPS, POS_ROWS, TS // WIN_Q, B_REP, WIN_Q), np.float32)
    for g in range(B_GROUPS):
        for r in range(B_REP):
            pieces = _bf16_pieces(B_SLOPES[g * B_REP + r] * LOG2E, 3)
            for i, piece in enumerate(pieces):
                slope[g, 2 * i:2 * i + 2, :, r, :] = piece
    slope = slope.reshape(B_GROUPS, POS_ROWS, B_REP * TS)
    slopef = np.zeros((B_GROUPS, 1, TS // WIN_Q, B_REP, WIN_Q), np.float32)
    for g in range(B_GROUPS):
        for r in range(B_REP):
            slopef[g, 0, :, r, :] = B_SLOPES[g * B_REP + r] * LOG2E
    slopef = slopef.reshape(B_GROUPS, 1, B_REP * TS)
    tpos = np.arange(seq)
    possel = _position_tile(tpos, tpos // SEL_LEN)
    poswin = _position_tile(np.arange(seq + WINDOW) - WINDOW)
    poscmp = _position_tile(np.arange(n_cmp) * CMP_STRIDE + (CMP_STRIDE - 1))
    return (jnp.asarray(ov, F32), jnp.asarray(slope, F32), jnp.asarray(slopef), possel, poswin, poscmp)


def _perm_heads(w):
    lead = w.shape[:-1]
    return w.reshape(lead + (B_HEADS, B_HEAD_DIM))[..., B_HEAD_PERM, :].reshape(lead + (WIDTH,))


def _cmp_weights(w):
    w = w.reshape(2, CMP_PHASES, B_HEAD_DIM, B_HEAD_DIM)
    eye = jnp.eye(B_GROUPS, dtype=w.dtype)
    big = jnp.einsum('spde,gh->spgdhe', w, eye).reshape(2, CMP_PHASES * B_KV, B_KV)
    return jnp.concatenate([big[0], big[1]], axis=1).astype(BF16)


def _cmp_pos(pos):
    p = jnp.tile(pos.reshape(2, CMP_PHASES, 1, B_HEAD_DIM), (1, 1, B_GROUPS, 1)).reshape(2, CMP_PHASES * B_KV)
    return jnp.concatenate([p, jnp.zeros((14, CMP_PHASES * B_KV), p.dtype)], axis=0)


def _block_diag(w):
    eye = jnp.eye(C_BLOCKS, dtype=w.dtype)
    return jnp.einsum('ncd,nm->ncmd', w, eye).reshape(WIDTH, WIDTH).astype(BF16)


def kernel(x, norm_g, w_in, lb_logits, a_norm_g, b_q_norm_g, b_k_norm_g, b_cmp_pos, b_cmp_wk, b_cmp_wv,
           c_conv_w, c_conv_b, c_w_ra, c_b_ra, c_w_ri, c_b_ri, c_lambda, d_norm_g, merge_b, w_branch, w_out):
    batch, seq, _ = x.shape
    depth = norm_g.shape[0]
    assert seq % TS == 0 and x.shape[-1] == D_MODEL

    p_lb = jax.nn.softmax(lb_logits.astype(F32), axis=0)
    lower_bounds = jnp.cumsum(p_lb, axis=0) - p_lb[0:1]
    dmat, kdec, qdec, cdec = _retention_tables()
    nsa_tabs = _nsa_tables(seq)
    row = lambda v: v.reshape(1, -1).astype(F32)
    vm = lambda shape, dt=F32: pltpu.VMEM(shape, dt)
    n_cmp = seq // CMP_STRIDE

    for l in range(depth):
        w = w_in[l]
        g = row(norm_g[l])
        w_a = w[:, 0:2048].astype(BF16)
        b0 = 2048
        gate_cols = jnp.pad(w[:, b0 + 1280:b0 + 1304], ((0, 0), (0, LANES - 3 * B_HEADS)))
        w_b = jnp.concatenate([_perm_heads(w[:, b0:b0 + 512]), w[:, b0 + 512:b0 + 1280], gate_cols,
                               _perm_heads(w[:, b0 + 1304:b0 + 1816])], axis=1).astype(BF16)
        c0 = b0 + 1816
        w_c = w[:, c0:c0 + 1024].astype(BF16)
        d0 = c0 + 1024
        w_d = w[:, d0:d0 + 1536].astype(BF16)
        m0 = d0 + 1536
        w_m = w[:, m0:m0 + N_BRANCH * D_MODEL].astype(BF16)

        o_a = _call(_hgrn_kernel, "hgrn", x, [g, w_a, row(lower_bounds[l]), row(a_norm_g[l])], WIDTH, BF16,
                    [vm((A_HEADS, A_HEAD_DIM, A_HEAD_DIM))])

        nsa_consts = [g, w_b, row(jnp.tile(b_q_norm_g[l], B_HEADS)), row(jnp.tile(b_k_norm_g[l], B_GROUPS)),
                      _cmp_pos(b_cmp_pos[l]), _cmp_weights(b_cmp_wk[l]), _cmp_weights(b_cmp_wv[l]), *nsa_tabs]
        nsa_scratch = [vm((TS, NSA_COLS)), vm((seq, B_KV), BF16), vm((seq // TS, B_KV, TS), BF16),
                       vm((seq + WINDOW, B_KV), BF16), vm(((seq + WINDOW) // Q_BLOCK, B_KV, Q_BLOCK), BF16),
                       vm((n_cmp, B_KV), BF16), vm((n_cmp, B_KV), BF16),
                       vm((TS // CMP_STRIDE, B_KV)), vm((TS // CMP_STRIDE, B_KV)), vm((2, TS, B_KV))]
        nsa = lambda online: _call(functools.partial(_nsa_kernel, seq, online), "nsa_online" if online else "nsa",
                                   x, nsa_consts, WIDTH, BF16, nsa_scratch)
        score_bound = (B_HEAD_DIM * B_HEAD_DIM ** -0.5 * LOG2E * jnp.max(jnp.abs(b_q_norm_g[l]))
                       * jnp.max(jnp.abs(b_k_norm_g[l])))
        o_b = lax.cond(score_bound <= FIXED_SHIFT_RANGE, lambda: nsa(False), lambda: nsa(True))

        o_c = _call(_rglru_kernel, "rglru", x,
                    [g, w_c, c_conv_w[l].astype(F32), row(c_conv_b[l]), _block_diag(c_w_ra[l]), row(c_b_ra[l]),
                     _block_diag(c_w_ri[l]), row(c_b_ri[l]), row(c_lambda[l])], WIDTH, BF16,
                    [vm((TS + 2 * CONV_PAD, WIDTH)), vm((8, WIDTH))] + [vm((WIDTH // LANES, TS, LANES))] * 4)

        o_d = _call(_ret_kernel, "retention", x, [g, w_d, dmat, kdec, qdec, cdec, row(d_norm_g[l])], WIDTH, BF16,
                    [vm((D_HEADS, LANES, D_V))])

        wbr = w_branch[l]
        wbr = jnp.stack([wbr[0], _perm_heads(wbr[1].T).T, wbr[2], wbr[3]], axis=0).astype(BF16)
        x = _call(_merge_kernel, "merge", x, [g, w_m, row(merge_b[l]), wbr, w_out[l].astype(BF16)],
                  D_MODEL, x.dtype, [], extra_tiles=(o_a, o_b, o_c, o_d))
    return x
```

```python
import functools

import numpy as np
import jax
import jax.numpy as jnp
from jax import lax
from jax.experimental import pallas as pl
from jax.experimental.pallas import tpu as pltpu

F32 = jnp.float32
BF16 = jnp.bfloat16

D_MODEL = 1024
WIDTH = 512
EPS = 1e-6
NEG_INF = -1e30
BIG = 1e30

A_HEADS = 4
A_HEAD_DIM = 128
CHUNK = 64

B_HEADS = 8
B_GROUPS = 2
B_REP = B_HEADS // B_GROUPS
B_HEAD_DIM = 64
B_KV = B_GROUPS * B_HEAD_DIM
CMP_LEN = 32
CMP_STRIDE = 16
SEL_LEN = 64
SEL_TOPN = 8
WINDOW = 256
Q_BLOCK = 128

C_BLOCKS = 8
C_BLOCK_DIM = 64
CONV_WIDTH = 4
RG_C = 8.0

D_HEADS = 4
D_QK = 64
D_V = 128

N_BRANCH = 4

TS = 512
LANES = 128
VMEM_LIMIT = 56 * 1024 * 1024
LOG2E = 1.4426950408889634

B_HEAD_PERM = (0, 4, 1, 5, 2, 6, 3, 7)
B_SLOPES = tuple(float(2.0 ** (-8.0 * (h + 1) / B_HEADS)) for h in range(B_HEADS))


def _dot(a, b):
    return jnp.dot(a.astype(BF16), b.astype(BF16), preferred_element_type=F32)


def _dot_nt(a, b):
    return lax.dot_general(a.astype(BF16), b.astype(BF16), (((1,), (1,)), ((), ())),
                           preferred_element_type=F32)


def _dot_tn(a, b):
    return lax.dot_general(a.astype(BF16), b.astype(BF16), (((0,), (0,)), ((), ())),
                           preferred_element_type=F32)


def _rms_rows(x, g):
    ms = jnp.mean(x * x, axis=-1, keepdims=True)
    return x * lax.rsqrt(ms + EPS) * g


def _sigmoid(x):
    return 0.5 * jnp.tanh(0.5 * x) + 0.5


def _silu(z):
    return z * _sigmoid(z)


def _iota(shape, dim):
    return lax.broadcasted_iota(jnp.int32, shape, dim)


def _normed_input(x_ref, g_ref):
    return _rms_rows(x_ref[...], g_ref[...]).astype(BF16)


N_CHUNKS = TS // CHUNK
INTRA = 256


def _chunk_broadcast(v, row):
    v3 = v.reshape(N_CHUNKS, CHUNK, v.shape[-1])
    return jnp.broadcast_to(v3[:, row:row + 1, :], v3.shape).reshape(v.shape)


def _chunk_rows(v, row):
    return v.reshape(N_CHUNKS, CHUNK, v.shape[-1])[:, row, :]


def _chunk_block_diag(v):
    n = v.shape[0] // CHUNK
    zero = jnp.zeros((CHUNK, v.shape[-1]), v.dtype)
    return jnp.concatenate(
        [jnp.concatenate([zero] * c + [v[c * CHUNK:(c + 1) * CHUNK]] + [zero] * (n - 1 - c), axis=1)
         for c in range(n)], axis=0)


def _chunk_diag_blocks(v, width):
    n = v.shape[0] // CHUNK
    return jnp.concatenate([v[c * CHUNK:(c + 1) * CHUNK, c * width:(c + 1) * width] for c in range(n)], axis=0)


def _intra_rows(i):
    return slice(i * INTRA, (i + 1) * INTRA)


def _hgrn_kernel(x_ref, g_ref, w_ref, lb_ref, ng_ref, o_ref, st_s):
    @pl.when(pl.program_id(1) == 0)
    def _():
        st_s[...] = jnp.zeros_like(st_s)

    u = jnp.dot(_normed_input(x_ref, g_ref), w_ref[...], preferred_element_type=F32)
    q = u[:, 0:WIDTH]
    fl = u[:, WIDTH:2 * WIDTH]
    lb = lb_ref[...]
    f = lb + (1.0 - lb) * jax.nn.sigmoid(fl)
    k = (1.0 - lb) * jax.nn.sigmoid(-fl)
    vb = u[:, 2 * WIDTH:3 * WIDTH].astype(BF16)
    b = jnp.log(f)
    row_in_chunk = _iota((TS, WIDTH), 0) & (CHUNK - 1)
    s = 1
    while s < CHUNK:
        b = b + jnp.where(row_in_chunk >= s, pltpu.roll(b, s, 0), 0.0)
        s *= 2
    b_mid = _chunk_broadcast(b, CHUNK // 2 - 1)
    b_end = _chunk_broadcast(b, CHUNK - 1)
    qe = (q * jnp.exp(b - b_mid)).astype(BF16)
    ke = (k * jnp.exp(b_mid - b)).astype(BF16)
    k_end = (k * jnp.exp(b_end - b)).astype(BF16)
    qb = (q * jnp.exp(b)).astype(BF16)
    dec = jnp.exp(_chunk_rows(b, CHUNK - 1))

    ri, ci = _iota((INTRA, INTRA), 0), _iota((INTRA, INTRA), 1)
    block_causal = jnp.logical_and(ri >= ci, (ri >> 6) == (ci >> 6))
    ng = ng_ref[...]
    for h in range(A_HEADS):
        sl = slice(h * A_HEAD_DIM, (h + 1) * A_HEAD_DIM)
        intra = []
        for i in range(TS // INTRA):
            rows = slice(i * INTRA, (i + 1) * INTRA)
            sc = jnp.where(block_causal, _dot_nt(qe[rows, sl], ke[rows, sl]), 0.0)
            intra.append(_dot(sc, vb[rows, sl]))
        kv_all = jnp.concatenate([_dot_tn(vb[_intra_rows(i), sl], _chunk_block_diag(k_end[_intra_rows(i), sl]))
                                  for i in range(TS // INTRA)], axis=1)
        st = st_s[h]
        states = []
        for c in range(N_CHUNKS):
            states.append(st.astype(BF16))
            st = st * dec[c:c + 1, sl] + kv_all[:, c * A_HEAD_DIM:(c + 1) * A_HEAD_DIM]
        st_s[h] = st
        per = INTRA // CHUNK
        inter = jnp.concatenate(
            [_chunk_diag_blocks(_dot_nt(qb[_intra_rows(i), sl], jnp.concatenate(states[i * per:(i + 1) * per], axis=0)),
                                A_HEAD_DIM) for i in range(TS // INTRA)], axis=0)
        o = jnp.concatenate(intra, axis=0) + inter
        z = u[:, 3 * WIDTH + h * A_HEAD_DIM:3 * WIDTH + (h + 1) * A_HEAD_DIM]
        o_ref[:, sl] = (_rms_rows(o, ng[:, sl]) * _silu(z)).astype(o_ref.dtype)


def _ret_kernel(x_ref, g_ref, w_ref, dmat_ref, kdec_ref, qdec_ref, cdec_ref, ng_ref, o_ref, st_s):
    @pl.when(pl.program_id(1) == 0)
    def _():
        st_s[...] = jnp.zeros_like(st_s)

    u = jnp.dot(_normed_input(x_ref, g_ref), w_ref[...], preferred_element_type=F32)
    nqk = D_HEADS * D_QK
    q = u[:, 0:nqk] * (D_QK ** -0.5)
    k = u[:, nqk:2 * nqk]
    kb = k.astype(BF16)
    kd = (k * kdec_ref[...]).astype(BF16)
    vb = u[:, 2 * nqk:2 * nqk + WIDTH].astype(BF16)
    first_half = _iota((TS, LANES), 1) < D_QK
    qdec = qdec_ref[...]
    cdec = cdec_ref[...]
    ng = ng_ref[...]
    for h in range(D_HEADS):
        pair = slice((h // 2) * LANES, (h // 2 + 1) * LANES)
        vsl = slice(h * D_V, (h + 1) * D_V)
        qm = jnp.where(first_half if h % 2 == 0 else jnp.logical_not(first_half), q[:, pair], 0.0).astype(BF16)
        intra = []
        for i in range(TS // INTRA):
            rows = slice(i * INTRA, (i + 1) * INTRA)
            sc = _dot_nt(qm[rows], kb[rows, pair]) * dmat_ref[h]
            intra.append(_dot(sc, vb[rows, vsl]))
        kv_all = jnp.concatenate([_dot_tn(kd[_intra_rows(i), pair], _chunk_block_diag(vb[_intra_rows(i), vsl]))
                                  for i in range(TS // INTRA)], axis=1)
        st = st_s[h]
        states = []
        for c in range(N_CHUNKS):
            states.append(st.astype(BF16))
            st = st * cdec[:, vsl] + kv_all[:, c * D_V:(c + 1) * D_V]
        st_s[h] = st
        per = INTRA // CHUNK
        inter = jnp.concatenate(
            [_chunk_diag_blocks(_dot(qm[_intra_rows(i)], jnp.concatenate(states[i * per:(i + 1) * per], axis=1)), D_V)
             for i in range(TS // INTRA)], axis=0)
        o = jnp.concatenate(intra, axis=0) + inter * qdec[:, vsl]
        oc = o - jnp.mean(o, axis=-1, keepdims=True)
        var = jnp.mean(oc * oc, axis=-1, keepdims=True)
        z = u[:, 2 * nqk + WIDTH + h * D_V:2 * nqk + WIDTH + (h + 1) * D_V]
        o_ref[:, vsl] = (oc * lax.rsqrt(var + EPS) * ng[:, vsl] * _silu(z)).astype(o_ref.dtype)


CONV_PAD = 8
GROUP = 8


def _scan_steps(a, b, row, length):
    s = 1
    while s < length:
        keep = row >= s
        a_prev = jnp.where(keep, pltpu.roll(a, s, 0), 1.0)
        b_prev = jnp.where(keep, pltpu.roll(b, s, 0), 0.0)
        b = a * b_prev + b
        a = a * a_prev
        s *= 2
    return a, b


def _rglru_kernel(x_ref, g_ref, w_ref, cw_ref, cb_ref, wra_ref, bra_ref, wri_ref, bri_ref, lam_ref,
                  o_ref, ext_s, h_s, a_s, b_s, ca_s, cb_s):
    t = pl.program_id(1)

    @pl.when(t == 0)
    def _():
        ext_s[0:CONV_PAD, :] = jnp.zeros((CONV_PAD, WIDTH), F32)
        h_s[...] = jnp.zeros_like(h_s)

    @pl.when(t > 0)
    def _():
        ext_s[0:CONV_PAD, :] = ext_s[TS:TS + CONV_PAD, :]

    u = jnp.dot(_normed_input(x_ref, g_ref), w_ref[...], preferred_element_type=F32)
    ext_s[CONV_PAD:CONV_PAD + TS, :] = u[:, 0:WIDTH]
    cz = u[:, WIDTH:2 * WIDTH]
    cw = cw_ref[...]
    xc = cb_ref[...] + cw[CONV_WIDTH - 1:CONV_WIDTH, :] * u[:, 0:WIDTH]
    for j in range(CONV_WIDTH - 1):
        back = CONV_WIDTH - 1 - j
        xc = xc + cw[j:j + 1, :] * ext_s[CONV_PAD - back:CONV_PAD - back + TS, :]
    r = _sigmoid(_dot(xc, wra_ref[...]) + bra_ref[...])
    ig = _sigmoid(_dot(xc, wri_ref[...]) + bri_ref[...])
    neg_lam = -lam_ref[...]
    softplus = jnp.maximum(neg_lam, 0.0) + jnp.log1p(jnp.exp(-jnp.abs(neg_lam)))
    log_a = -RG_C * r * softplus
    a = jnp.exp(log_a)
    b = jnp.sqrt(1.0 - a * a) * (ig * xc)
    n_groups = TS // GROUP
    planes = WIDTH // LANES
    rows_of = lambda r: pl.ds(r, n_groups, stride=GROUP)
    for j in range(planes):
        a_s[j] = a[:, j * LANES:(j + 1) * LANES]
        b_s[j] = b[:, j * LANES:(j + 1) * LANES]
    cum_a = [a_s[j, rows_of(0), :] for j in range(planes)]
    cum_b = [b_s[j, rows_of(0), :] for j in range(planes)]
    for r in range(GROUP):
        for j in range(planes):
            if r > 0:
                a_r = a_s[j, rows_of(r), :]
                cum_b[j] = a_r * cum_b[j] + b_s[j, rows_of(r), :]
                cum_a[j] = a_r * cum_a[j]
            ca_s[j, rows_of(r), :] = cum_a[j]
            cb_s[j, rows_of(r), :] = cum_b[j]
    group = _iota((n_groups, WIDTH), 0)
    a_tot, b_tot = _scan_steps(jnp.concatenate(cum_a, axis=1), jnp.concatenate(cum_b, axis=1), group, n_groups)
    h0 = h_s[0:1, :]
    h_after = a_tot * h0 + b_tot
    h_before = jnp.where(group == 0, h0, pltpu.roll(h_after, 1, 0))
    h_s[0:1, :] = h_after[n_groups - 1:n_groups, :]
    for r in range(GROUP):
        for j in range(planes):
            lanes = slice(j * LANES, (j + 1) * LANES)
            a_s[j, rows_of(r), :] = ca_s[j, rows_of(r), :] * h_before[:, lanes] + cb_s[j, rows_of(r), :]
    h = jnp.concatenate([a_s[j] for j in range(planes)], axis=1)
    o_ref[...] = (h * _silu(cz)).astype(o_ref.dtype)


NSA_Q = 0
NSA_KC = 512
NSA_VC = 640
NSA_KS = 768
NSA_VS = 896
NSA_KW = 1024
NSA_VW = 1152
NSA_G = 1280
NSA_Z = 1408
NSA_COLS = 1920
SEL_KEYS = TS
WIN_Q = 256
WIN_KEYS = Q_BLOCK + WINDOW
CMP_PHASES = CMP_STRIDE
AUG = 2 * LANES
POS_ROWS = 16
MASK_BIG = float(2.0 ** 40)
SUM_ROWS = 16
SEL_COLS = 256
SEL_AHEAD = 3
T_ROWS = (6, 7, 8)
FIXED_SHIFT_RANGE = 60.0


def _half_rms(v, first_half):
    v2 = v * v
    s0 = jnp.sum(jnp.where(first_half, v2, 0.0), axis=-1, keepdims=True)
    s1 = jnp.sum(jnp.where(first_half, 0.0, v2), axis=-1, keepdims=True)
    ms = jnp.where(first_half, s0, s1) * (1.0 / B_HEAD_DIM)
    return v * lax.rsqrt(ms + EPS)


def _rank_select(imp, n_live):
    n_sel, n_query = imp.shape
    cnt = []
    for v0 in range(0, n_live, 8):
        mine, jmine = imp[v0:v0 + 8], v0 + _iota((8, n_query), 0)
        c = jnp.zeros((8, n_query), F32)
        for jp in range(n_live):
            other = imp[jp:jp + 1, :]
            if jp < v0:
                ahead = other >= mine
            elif jp >= v0 + 8:
                ahead = other > mine
            else:
                ahead = jnp.where(jmine > jp, jnp.where(other >= mine, 1.0, 0.0),
                                  jnp.where(other > mine, 1.0, 0.0)) > 0.5
            c = c + jnp.where(ahead, 1.0, 0.0)
        cnt.append(jnp.where(c < SEL_TOPN, 0.0, -MASK_BIG))
    if n_live < n_sel:
        cnt.append(jnp.full((n_sel - n_live, n_query), -MASK_BIG, F32))
    return jnp.concatenate(cnt, axis=0).astype(BF16)


def _softmax_cols(s):
    p = jnp.exp2(s - jnp.max(s, axis=0, keepdims=True))
    return p, jnp.sum(p, axis=0, keepdims=True)


def _nsa_kernel(seq, online, x_ref, g_ref, w_ref, qg_ref, kg_ref, pos12_ref, w12k_ref, w12v_ref,
                ov_ref, slope_ref, slopef_ref, possel_ref, poswin_ref, poscmp_ref, o_ref,
                u_s, ks_c, vst_c, kw_c, vwt_c, kc_c, vc_c, p1k_s, p1v_s, kv_s):
    n_cmp = seq // CMP_STRIDE
    n_sel = seq // SEL_LEN
    ti = pl.program_id(1)

    @pl.when(ti == 0)
    def _():
        for ref in (ks_c, vst_c, kw_c, vwt_c, kc_c, vc_c, p1k_s, p1v_s):
            ref[...] = jnp.zeros_like(ref)

    u_s[...] = jnp.dot(_normed_input(x_ref, g_ref), w_ref[...], preferred_element_type=F32)
    row0 = pl.multiple_of(ti * TS, TS)
    kg = kg_ref[...]
    half_t = _iota((TS, LANES), 1) < B_HEAD_DIM

    ks_c[pl.ds(row0, TS), :] = (_half_rms(u_s[:, NSA_KS:NSA_KS + B_KV], half_t) * kg).astype(BF16)
    vst_c[ti] = u_s[:, NSA_VS:NSA_VS + B_KV].T.astype(BF16)
    kw_c[pl.ds(row0 + WINDOW, TS), :] = (_half_rms(u_s[:, NSA_KW:NSA_KW + B_KV], half_t) * kg).astype(BF16)
    for i in range(TS // Q_BLOCK):
        vwt_c[ti * (TS // Q_BLOCK) + WINDOW // Q_BLOCK + i] = (
            u_s[i * Q_BLOCK:(i + 1) * Q_BLOCK, NSA_VW:NSA_VW + B_KV].T.astype(BF16))

    n_grp = TS // CMP_STRIDE
    first_row = _iota((n_grp, B_KV), 0) == 0
    half_g = _iota((n_grp, LANES), 1) < B_HEAD_DIM
    crow = pl.ds(pl.multiple_of(ti * n_grp, n_grp), n_grp)
    for i, (col, w12_ref, p1_s, cache, is_key) in enumerate(((NSA_KC, w12k_ref, p1k_s, kc_c, True),
                                                             (NSA_VC, w12v_ref, p1v_s, vc_c, False))):
        kv_s[i] = u_s[:, col:col + B_KV]
        grouped = jnp.concatenate([kv_s[i, pl.ds(r, n_grp, stride=CMP_STRIDE), :] for r in range(CMP_PHASES)],
                                  axis=1)
        both = _dot(jnp.concatenate([grouped, pos12_ref[...]], axis=0), w12_ref[...])
        p1 = both[0:n_grp, 0:B_KV] + both[n_grp:n_grp + 1, 0:B_KV]
        p2 = both[0:n_grp, B_KV:2 * B_KV] + both[n_grp + 1:n_grp + 2, B_KV:2 * B_KV]
        p1_prev = jnp.where(first_row, p1_s[n_grp - 1:n_grp, :], pltpu.roll(p1, 1, 0))
        p1_s[...] = p1
        blk = p1_prev + p2
        if is_key:
            blk = _half_rms(blk, half_g) * kg
        cache[crow, :] = blk.astype(BF16)

    qg = qg_ref[...]
    n_q = B_REP * TS
    halves = TS // WIN_Q

    def cols(a):
        return jnp.concatenate([a[:, hb * WIN_Q:(hb + 1) * WIN_Q] for hb in range(halves) for _ in range(B_REP)],
                               axis=1)

    def col_block(hb, r):
        return slice((hb * B_REP + r) * WIN_Q, (hb * B_REP + r + 1) * WIN_Q)

    q_n = [_half_rms(u_s[:, NSA_Q + r * LANES:NSA_Q + (r + 1) * LANES], half_t)
           * qg[:, r * LANES:(r + 1) * LANES] * (B_HEAD_DIM ** -0.5 * LOG2E) for r in range(B_REP)]
    q_t = jnp.concatenate([q_n[r][hb * WIN_Q:(hb + 1) * WIN_Q].T for hb in range(halves) for r in range(B_REP)],
                          axis=1)
    q_rows_first = _iota((LANES, n_q), 0) < B_HEAD_DIM
    t_col = (row0 + cols(_iota((1, TS), 1))).astype(F32)
    prow = _iota((POS_ROWS, n_q), 0)
    slope_rows = []
    for g in range(B_GROUPS):
        rest = -slopef_ref[g] * t_col
        block = slope_ref[g]
        for row in T_ROWS:
            piece = rest.astype(BF16).astype(F32)
            block = block + jnp.where(prow == row, piece, 0.0)
            rest = rest - piece
        slope_rows.append(block.astype(BF16))
    base = [jnp.concatenate([jnp.where(q_rows_first, q_t, 0.0).astype(BF16), slope_rows[0]], axis=0),
            jnp.concatenate([jnp.where(q_rows_first, 0.0, q_t).astype(BF16), slope_rows[1]], axis=0)]
    n_base = LANES + POS_ROWS
    groups = range(B_GROUPS)

    t_c = row0 + _iota((n_cmp, TS), 1)
    nprime = _iota((n_cmp, TS), 0)
    madd_c = cols(jnp.where(t_c >= nprime * CMP_STRIDE + (CMP_STRIDE - 1),
                            jnp.where(nprime >= 1, 0.0, NEG_INF), NEG_INF))
    any_c = cols(jnp.where(row0 + _iota((1, TS), 1) >= CMP_LEN - 1, 1.0, 0.0))
    kaug_c = jnp.concatenate([kc_c[...], poscmp_ref[...]], axis=1)
    vct = vc_c[...].astype(F32).T.astype(BF16)
    zeros_c = jnp.zeros((AUG - n_base, n_q), BF16)
    s_c = [jnp.dot(kaug_c, jnp.concatenate([base[g], zeros_c], axis=0), preferred_element_type=F32) + madd_c
           for g in groups]
    p_c = []
    for g in groups:
        p, l = _softmax_cols(s_c[g])
        p_c.append(p * (any_c / l))
    o_cmp = [_dot(vct[g * B_HEAD_DIM:(g + 1) * B_HEAD_DIM], p_c[g]) for g in groups]
    blk_t = (row0 + _iota((n_sel, TS), 1)) >> 6
    jrow = _iota((n_sel, TS), 0)
    q_aug = []
    for g in groups:
        p_sum = jnp.concatenate(
            [sum(p_c[g][:, col_block(hb, r)] for r in range(B_REP)) for hb in range(halves)], axis=1)
        imp = jnp.dot(ov_ref[...], p_sum, precision=lax.Precision.HIGHEST,
                      preferred_element_type=F32)
        imp = jnp.where(jrow == blk_t, BIG, jnp.where(jrow < blk_t, imp, -BIG))
        sel_neg = lax.switch(ti, [functools.partial(_rank_select, n_live=(k + 1) * (TS // SEL_LEN))
                                  for k in range(seq // TS)], imp)
        q_aug.append(jnp.concatenate([base[g], cols(sel_neg),
                                      jnp.zeros((AUG - n_base - n_sel, n_q), BF16)], axis=0))

    def sel_step(kc, st, diagonal):
        k0 = pl.multiple_of(kc * SEL_KEYS, SEL_KEYS)
        kaug = jnp.concatenate([ks_c[pl.ds(k0, SEL_KEYS), :], possel_ref[pl.ds(k0, SEL_KEYS), :]], axis=1)
        ones = jnp.ones((SUM_ROWS, SEL_KEYS), BF16)
        v_aug = [jnp.concatenate([vst_c[kc, g * B_HEAD_DIM:(g + 1) * B_HEAD_DIM, :], ones], axis=0) for g in groups]
        if diagonal:
            madd = cols(jnp.where(_iota((SEL_KEYS, TS), 0) <= _iota((SEL_KEYS, TS), 1), 0.0, NEG_INF))
        units = [(g, slice(b * SEL_COLS, (b + 1) * SEL_COLS)) for g in groups for b in range(n_q // SEL_COLS)]

        def n_keys(cs):
            return (cs.start // (B_REP * WIN_Q) + 1) * WIN_Q if diagonal else SEL_KEYS

        def qk(g, cs):
            n = n_keys(cs)
            s = jnp.dot(kaug[0:n], q_aug[g][:, cs], preferred_element_type=F32)
            return s + madd[0:n, cs] if diagonal else s

        scores = [qk(*u) for u in units[:SEL_AHEAD]]
        m_out, acc_out = [[] for _ in groups], [[] for _ in groups]
        for i, (g, cs) in enumerate(units):
            if i + SEL_AHEAD < len(units):
                scores.append(qk(*units[i + SEL_AHEAD]))
            s = scores[i]
            m_old, acc_old = st[2 * g][:, cs], st[2 * g + 1][:, cs]
            if online:
                m_new = jnp.maximum(m_old, jnp.max(s, axis=0, keepdims=True))
                s = s - m_new
                acc_old = jnp.exp2(m_old - m_new) * acc_old
            else:
                m_new = m_old
            p = jnp.exp2(s.astype(BF16))
            m_out[g].append(m_new)
            acc_out[g].append(acc_old + jnp.dot(v_aug[g][:, 0:n_keys(cs)], p, preferred_element_type=F32))
        out = []
        for g in groups:
            out += [jnp.concatenate(m_out[g], axis=1), jnp.concatenate(acc_out[g], axis=1)]
        return tuple(out)

    init = []
    for g in groups:
        init += [jnp.full((1, n_q), NEG_INF, F32), jnp.zeros((B_HEAD_DIM + SUM_ROWS, n_q), F32)]
    st = lax.fori_loop(0, ti, functools.partial(sel_step, diagonal=False), tuple(init))
    st = sel_step(ti, st, diagonal=True)
    o_sel = [st[2 * g + 1][0:B_HEAD_DIM] / st[2 * g + 1][B_HEAD_DIM:B_HEAD_DIM + 1] for g in groups]

    per_half = WIN_Q // Q_BLOCK
    win_ones = jnp.ones((SUM_ROWS, WIN_KEYS), BF16)
    head_pairs = B_REP // 2

    def win_scores(qi, g, hp):
        hb, qq = divmod(qi, per_half)
        w0 = row0 + qi * Q_BLOCK
        kaug_w = jnp.concatenate([kw_c[pl.ds(w0, WIN_KEYS), :], poswin_ref[pl.ds(w0, WIN_KEYS), :]], axis=1)
        spos = (w0 - WINDOW) + _iota((WIN_KEYS, Q_BLOCK), 0)
        dist_w = (w0 + _iota((WIN_KEYS, Q_BLOCK), 1)) - spos
        madd_w = jnp.where(dist_w >= 0, jnp.where(dist_w < WINDOW, jnp.where(spos >= 0, 0.0, NEG_INF), NEG_INF),
                           NEG_INF)
        q_w = jnp.concatenate([q_aug[g][:, col_block(hb, r).start + qq * Q_BLOCK:
                                        col_block(hb, r).start + (qq + 1) * Q_BLOCK]
                               for r in (2 * hp, 2 * hp + 1)], axis=1)
        return (jnp.dot(kaug_w, q_w, preferred_element_type=F32)
                + jnp.concatenate([madd_w] * 2, axis=1))

    win_units = [(qi, g, hp) for qi in range(TS // Q_BLOCK) for hp in range(head_pairs) for g in groups]
    win_s = [win_scores(*u) for u in win_units[:SEL_AHEAD]]
    o_win_unit = {}
    for i, (qi, g, hp) in enumerate(win_units):
        if i + SEL_AHEAD < len(win_units):
            win_s.append(win_scores(*win_units[i + SEL_AHEAD]))
        s = win_s[i]
        if online:
            s = s - jnp.max(s, axis=0, keepdims=True)
        wblk = ti * (TS // Q_BLOCK) + qi
        vwt = jnp.concatenate(
            [jnp.concatenate([vwt_c[wblk + j, g * B_HEAD_DIM:(g + 1) * B_HEAD_DIM, :]
                              for j in range(WIN_KEYS // Q_BLOCK)], axis=1), win_ones], axis=0)
        pv = jnp.dot(vwt, jnp.exp2(s.astype(BF16)), preferred_element_type=F32)
        o_win_unit[(qi, g, hp)] = pv[0:B_HEAD_DIM] / pv[B_HEAD_DIM:B_HEAD_DIM + 1]
    o_win = [jnp.concatenate(
        [o_win_unit[(hb * per_half + qq, g, r // 2)][:, (r % 2) * Q_BLOCK:(r % 2 + 1) * Q_BLOCK]
         for hb in range(halves) for r in range(B_REP) for qq in range(per_half)], axis=1) for g in groups]

    gates_t = _sigmoid(u_s[:, NSA_G:NSA_G + LANES]).T
    for hb in range(halves):
        ts = slice(hb * WIN_Q, (hb + 1) * WIN_Q)
        for r in range(B_REP):
            cs = col_block(hb, r)
            acc = jnp.zeros((LANES, WIN_Q), F32)
            for c, branch in enumerate((o_cmp, o_sel, o_win)):
                acc = acc + jnp.concatenate(
                    [branch[0][:, cs] * gates_t[r * 3 + c:r * 3 + c + 1, ts],
                     branch[1][:, cs] * gates_t[(B_REP + r) * 3 + c:(B_REP + r) * 3 + c + 1, ts]], axis=0)
            z = u_s[ts, NSA_Z + r * LANES:NSA_Z + (r + 1) * LANES]
            o_ref[ts, r * LANES:(r + 1) * LANES] = (acc.T * _silu(z)).astype(o_ref.dtype)


def _merge_kernel(x_ref, oa_ref, ob_ref, oc_ref, od_ref, g_ref, wmg_ref, mb_ref, wbr_ref, wout_ref, o_ref):
    x = x_ref[...]
    xn = _rms_rows(x, g_ref[...]).astype(BF16)
    merged = jnp.zeros((TS, D_MODEL), F32)
    for br, o_k in enumerate((oa_ref, ob_ref, oc_ref, od_ref)):
        cols = slice(br * D_MODEL, (br + 1) * D_MODEL)
        gate = _sigmoid(jnp.dot(xn, wmg_ref[:, cols], preferred_element_type=F32) + mb_ref[:, cols])
        merged = merged + gate * jnp.dot(o_k[...], wbr_ref[br], preferred_element_type=F32)
    o_ref[...] = x + jnp.dot(merged.astype(BF16), wout_ref[...], preferred_element_type=F32)


def _const_spec(shape):
    nd = len(shape)
    return pl.BlockSpec(shape, lambda b, t: (0,) * nd, pipeline_mode=pl.Buffered(1))


def _tile_spec(width):
    return pl.BlockSpec((None, TS, width), lambda b, t: (b, t, 0))


def _call(kernel, name, x, consts, out_width, out_dtype, scratch, extra_tiles=()):
    batch, seq, _ = x.shape
    in_specs = [_tile_spec(D_MODEL)] + [_tile_spec(a.shape[-1]) for a in extra_tiles]
    in_specs += [_const_spec(c.shape) for c in consts]
    return pl.pallas_call(
        kernel,
        name=name,
        grid=(batch, seq // TS),
        in_specs=in_specs,
        out_specs=_tile_spec(out_width),
        out_shape=jax.ShapeDtypeStruct((batch, seq, out_width), out_dtype),
        scratch_shapes=scratch,
        compiler_params=pltpu.CompilerParams(
            dimension_semantics=("arbitrary", "arbitrary"), vmem_limit_bytes=VMEM_LIMIT),
    )(x, *extra_tiles, *consts)


def _retention_tables():
    pos = np.arange(TS, dtype=np.float64)
    log_g = np.log1p(-np.exp2(-5.0 - np.arange(D_HEADS, dtype=np.float64)))
    rel = pos[:, None] - pos[None, :]
    same = (pos[:, None] // CHUNK) == (pos[None, :] // CHUNK)
    dmat = np.where((rel >= 0) & same, np.exp(log_g[:, None, None] * np.maximum(rel, 0.0)), 0.0)
    dmat = dmat[:, :INTRA, :INTRA]
    inpos = pos % CHUNK
    kdec = np.repeat(np.exp(log_g[None, :] * (CHUNK - 1.0 - inpos)[:, None]), D_QK, axis=1)
    qdec = np.repeat(np.exp(log_g[None, :] * (inpos + 1.0)[:, None]), D_V, axis=1)
    cdec = np.repeat(np.exp(log_g * CHUNK)[None, :], D_V, axis=1)
    return tuple(jnp.asarray(a, F32) for a in (dmat, kdec, qdec, cdec))


def _bf16_pieces(v, n):
    out = []
    rem = np.asarray(v, np.float64)
    for _ in range(n):
        piece = rem.astype(BF16).astype(np.float64)
        out.append(piece)
        rem = rem - piece
    return out


def _position_tile(pos, block_ids=None):
    pos = np.maximum(np.asarray(pos), 0)
    tab = np.zeros((pos.shape[0], LANES), np.float32)
    for i in range(3):
        tab[:, 2 * i] = (pos // 256) * 256
        tab[:, 2 * i + 1] = pos % 256
    tab[:, list(T_ROWS)] = 1.0
    if block_ids is not None:
        tab[np.arange(pos.shape[0]), POS_ROWS + np.asarray(block_ids)] = 1.0
    return jnp.asarray(tab, BF16)


def _nsa_tables(seq):
    n_cmp = seq // CMP_STRIDE
    n_sel = seq // SEL_LEN
    n = np.arange(n_cmp) - 1
    cmp_start = n * CMP_STRIDE
    sel_start = np.arange(n_sel) * SEL_LEN
    ov = ((cmp_start[None, :] < sel_start[:, None] + SEL_LEN) & (cmp_start[None, :] + CMP_LEN > sel_start[:, None])
          & (n[None, :] >= 0) & (n[None, :] <= (seq - CMP_LEN) // CMP_STRIDE))
    slope = np.zeros((B_GROUPS, POS_ROWS, TS // WIN_Q, B_REP, WIN_Q), np.float32)
    for g in range(B_GROUPS):
        for r in range(B_REP):
            pieces = _bf16_pieces(B_SLOPES[g * B_REP + r] * LOG2E, 3)
            for i, piece in enumerate(pieces):
                slope[g, 2 * i:2 * i + 2, :, r, :] = piece
    slope = slope.reshape(B_GROUPS, POS_ROWS, B_REP * TS)
    slopef = np.zeros((B_GROUPS, 1, TS // WIN_Q, B_REP, WIN_Q), np.float32)
    for g in range(B_GROUPS):
        for r in range(B_REP):
            slopef[g, 0, :, r, :] = B_SLOPES[g * B_REP + r] * LOG2E
    slopef = slopef.reshape(B_GROUPS, 1, B_REP * TS)
    tpos = np.arange(seq)
    possel = _position_tile(tpos, tpos // SEL_LEN)
    poswin = _position_tile(np.arange(seq + WINDOW) - WINDOW)
    poscmp = _position_tile(np.arange(n_cmp) * CMP_STRIDE + (CMP_STRIDE - 1))
    return (jnp.asarray(ov, F32), jnp.asarray(slope, F32), jnp.asarray(slopef), possel, poswin, poscmp)


def _perm_heads(w):
    lead = w.shape[:-1]
    return w.reshape(lead + (B_HEADS, B_HEAD_DIM))[..., B_HEAD_PERM, :].reshape(lead + (WIDTH,))


def _cmp_weights(w):
    w = w.reshape(2, CMP_PHASES, B_HEAD_DIM, B_HEAD_DIM)
    eye = jnp.eye(B_GROUPS, dtype=w.dtype)
    big = jnp.einsum('spde,gh->spgdhe', w, eye).reshape(2, CMP_PHASES * B_KV, B_KV)
    return jnp.concatenate([big[0], big[1]], axis=1).astype(BF16)


def _cmp_pos(pos):
    p = jnp.tile(pos.reshape(2, CMP_PHASES, 1, B_HEAD_DIM), (1, 1, B_GROUPS, 1)).reshape(2, CMP_PHASES * B_KV)
    return jnp.concatenate([p, jnp.zeros((14, CMP_PHASES * B_KV), p.dtype)], axis=0)


def _block_diag(w):
    eye = jnp.eye(C_BLOCKS, dtype=w.dtype)
    return jnp.einsum('ncd,nm->ncmd', w, eye).reshape(WIDTH, WIDTH).astype(BF16)


def kernel(x, norm_g, w_in, lb_logits, a_norm_g, b_q_norm_g, b_k_norm_g, b_cmp_pos, b_cmp_wk, b_cmp_wv,
           c_conv_w, c_conv_b, c_w_ra, c_b_ra, c_w_ri, c_b_ri, c_lambda, d_norm_g, merge_b, w_branch, w_out):
    batch, seq, _ = x.shape
    depth = norm_g.shape[0]
    assert seq % TS == 0 and x.shape[-1] == D_MODEL

    p_lb = jax.nn.softmax(lb_logits.astype(F32), axis=0)
    lower_bounds = jnp.cumsum(p_lb, axis=0) - p_lb[0:1]
    dmat, kdec, qdec, cdec = _retention_tables()
    nsa_tabs = _nsa_tables(seq)
    row = lambda v: v.reshape(1, -1).astype(F32)
    vm = lambda shape, dt=F32: pltpu.VMEM(shape, dt)
    n_cmp = seq // CMP_STRIDE

    for l in range(depth):
        w = w_in[l]
        g = row(norm_g[l])
        w_a = w[:, 0:2048].astype(BF16)
        b0 = 2048
        gate_cols = jnp.pad(w[:, b0 + 1280:b0 + 1304], ((0, 0), (0, LANES - 3 * B_HEADS)))
        w_b = jnp.concatenate([_perm_heads(w[:, b0:b0 + 512]), w[:, b0 + 512:b0 + 1280], gate_cols,
                               _perm_heads(w[:, b0 + 1304:b0 + 1816])], axis=1).astype(BF16)
        c0 = b0 + 1816
        w_c = w[:, c0:c0 + 1024].astype(BF16)
        d0 = c0 + 1024
        w_d = w[:, d0:d0 + 1536].astype(BF16)
        m0 = d0 + 1536
        w_m = w[:, m0:m0 + N_BRANCH * D_MODEL].astype(BF16)

        o_a = _call(_hgrn_kernel, "hgrn", x, [g, w_a, row(lower_bounds[l]), row(a_norm_g[l])], WIDTH, BF16,
                    [vm((A_HEADS, A_HEAD_DIM, A_HEAD_DIM))])

        nsa_consts = [g, w_b, row(jnp.tile(b_q_norm_g[l], B_HEADS)), row(jnp.tile(b_k_norm_g[l], B_GROUPS)),
                      _cmp_pos(b_cmp_pos[l]), _cmp_weights(b_cmp_wk[l]), _cmp_weights(b_cmp_wv[l]), *nsa_tabs]
        nsa_scratch = [vm((TS, NSA_COLS)), vm((seq, B_KV), BF16), vm((seq // TS, B_KV, TS), BF16),
                       vm((seq + WINDOW, B_KV), BF16), vm(((seq + WINDOW) // Q_BLOCK, B_KV, Q_BLOCK), BF16),
                       vm((n_cmp, B_KV), BF16), vm((n_cmp, B_KV), BF16),
                       vm((TS // CMP_STRIDE, B_KV)), vm((TS // CMP_STRIDE, B_KV)), vm((2, TS, B_KV))]
        nsa = lambda online: _call(functools.partial(_nsa_kernel, seq, online), "nsa_online" if online else "nsa",
                                   x, nsa_consts, WIDTH, BF16, nsa_scratch)
        score_bound = (B_HEAD_DIM * B_HEAD_DIM ** -0.5 * LOG2E * jnp.max(jnp.abs(b_q_norm_g[l]))
                       * jnp.max(jnp.abs(b_k_norm_g[l])))
        o_b = lax.cond(score_bound <= FIXED_SHIFT_RANGE, lambda: nsa(False), lambda: nsa(True))

        o_c = _call(_rglru_kernel, "rglru", x,
                    [g, w_c, c_conv_w[l].astype(F32), row(c_conv_b[l]), _block_diag(c_w_ra[l]), row(c_b_ra[l]),
                     _block_diag(c_w_ri[l]), row(c_b_ri[l]), row(c_lambda[l])], WIDTH, BF16,
                    [vm((TS + 2 * CONV_PAD, WIDTH)), vm((8, WIDTH))] + [vm((WIDTH // LANES, TS, LANES))] * 4)

        o_d = _call(_ret_kernel, "retention", x, [g, w_d, dmat, kdec, qdec, cdec, row(d_norm_g[l])], WIDTH, BF16,
                    [vm((D_HEADS, LANES, D_V))])

        wbr = w_branch[l]
        wbr = jnp.stack([wbr[0], _perm_heads(wbr[1].T).T, wbr[2], wbr[3]], axis=0).astype(BF16)
        x = _call(_merge_kernel, "merge", x, [g, w_m, row(merge_b[l]), wbr, w_out[l].astype(BF16)],
                  D_MODEL, x.dtype, [], extra_tiles=(o_a, o_b, o_c, o_d))
    return x
```

```python
import functools

import numpy as np
import jax
import jax.numpy as jnp
from jax import lax
from jax.experimental import pallas as pl
from jax.experimental.pallas import tpu as pltpu

F32 = jnp.float32
BF16 = jnp.bfloat16

D_MODEL = 1024
WIDTH = 512
EPS = 1e-6
NEG_INF = -1e30
BIG = 1e30

A_HEADS = 4
A_HEAD_DIM = 128
CHUNK = 64

B_HEADS = 8
B_GROUPS = 2
B_REP = B_HEADS // B_GROUPS
B_HEAD_DIM = 64
B_KV = B_GROUPS * B_HEAD_DIM
CMP_LEN = 32
CMP_STRIDE = 16
SEL_LEN = 64
SEL_TOPN = 8
WINDOW = 256
Q_BLOCK = 128

C_BLOCKS = 8
C_BLOCK_DIM = 64
CONV_WIDTH = 4
RG_C = 8.0

D_HEADS = 4
D_QK = 64
D_V = 128

N_BRANCH = 4

TS = 512
LANES = 128
VMEM_LIMIT = 56 * 1024 * 1024
LOG2E = 1.4426950408889634

B_HEAD_PERM = (0, 4, 1, 5, 2, 6, 3, 7)
B_SLOPES = tuple(float(2.0 ** (-8.0 * (h + 1) / B_HEADS)) for h in range(B_HEADS))


def _dot(a, b):
    return jnp.dot(a.astype(BF16), b.astype(BF16), preferred_element_type=F32)


def _dot_nt(a, b):
    return lax.dot_general(a.astype(BF16), b.astype(BF16), (((1,), (1,)), ((), ())),
                           preferred_element_type=F32)


def _dot_tn(a, b):
    return lax.dot_general(a.astype(BF16), b.astype(BF16), (((0,), (0,)), ((), ())),
                           preferred_element_type=F32)


def _rms_rows(x, g):
    ms = jnp.mean(x * x, axis=-1, keepdims=True)
    return x * lax.rsqrt(ms + EPS) * g


def _sigmoid(x):
    return 0.5 * jnp.tanh(0.5 * x) + 0.5


def _silu(z):
    return z * _sigmoid(z)


def _iota(shape, dim):
    return lax.broadcasted_iota(jnp.int32, shape, dim)


def _normed_input(x_ref, g_ref):
    return _rms_rows(x_ref[...], g_ref[...]).astype(BF16)


N_CHUNKS = TS // CHUNK
INTRA = 256


def _chunk_broadcast(v, row):
    v3 = v.reshape(N_CHUNKS, CHUNK, v.shape[-1])
    return jnp.broadcast_to(v3[:, row:row + 1, :], v3.shape).reshape(v.shape)


def _chunk_rows(v, row):
    return v.reshape(N_CHUNKS, CHUNK, v.shape[-1])[:, row, :]


def _chunk_block_diag(v):
    n = v.shape[0] // CHUNK
    zero = jnp.zeros((CHUNK, v.shape[-1]), v.dtype)
    return jnp.concatenate(
        [jnp.concatenate([zero] * c + [v[c * CHUNK:(c + 1) * CHUNK]] + [zero] * (n - 1 - c), axis=1)
         for c in range(n)], axis=0)


def _chunk_diag_blocks(v, width):
    n = v.shape[0] // CHUNK
    return jnp.concatenate([v[c * CHUNK:(c + 1) * CHUNK, c * width:(c + 1) * width] for c in range(n)], axis=0)


def _intra_rows(i):
    return slice(i * INTRA, (i + 1) * INTRA)


def _hgrn_kernel(x_ref, g_ref, w_ref, lb_ref, ng_ref, o_ref, st_s):
    @pl.when(pl.program_id(1) == 0)
    def _():
        st_s[...] = jnp.zeros_like(st_s)

    u = jnp.dot(_normed_input(x_ref, g_ref), w_ref[...], preferred_element_type=F32)
    q = u[:, 0:WIDTH]
    fl = u[:, WIDTH:2 * WIDTH]
    lb = lb_ref[...]
    f = lb + (1.0 - lb) * jax.nn.sigmoid(fl)
    k = (1.0 - lb) * jax.nn.sigmoid(-fl)
    vb = u[:, 2 * WIDTH:3 * WIDTH].astype(BF16)
    b = jnp.log(f)
    row_in_chunk = _iota((TS, WIDTH), 0) & (CHUNK - 1)
    s = 1
    while s < CHUNK:
        b = b + jnp.where(row_in_chunk >= s, pltpu.roll(b, s, 0), 0.0)
        s *= 2
    b_mid = _chunk_broadcast(b, CHUNK // 2 - 1)
    b_end = _chunk_broadcast(b, CHUNK - 1)
    qe = (q * jnp.exp(b - b_mid)).astype(BF16)
    ke = (k * jnp.exp(b_mid - b)).astype(BF16)
    k_end = (k * jnp.exp(b_end - b)).astype(BF16)
    qb = (q * jnp.exp(b)).astype(BF16)
    dec = jnp.exp(_chunk_rows(b, CHUNK - 1))

    ri, ci = _iota((INTRA, INTRA), 0), _iota((INTRA, INTRA), 1)
    block_causal = jnp.logical_and(ri >= ci, (ri >> 6) == (ci >> 6))
    ng = ng_ref[...]
    for h in range(A_HEADS):
        sl = slice(h * A_HEAD_DIM, (h + 1) * A_HEAD_DIM)
        intra = []
        for i in range(TS // INTRA):
            rows = slice(i * INTRA, (i + 1) * INTRA)
            sc = jnp.where(block_causal, _dot_nt(qe[rows, sl], ke[rows, sl]), 0.0)
            intra.append(_dot(sc, vb[rows, sl]))
        kv_all = jnp.concatenate([_dot_tn(vb[_intra_rows(i), sl], _chunk_block_diag(k_end[_intra_rows(i), sl]))
                                  for i in range(TS // INTRA)], axis=1)
        st = st_s[h]
        states = []
        for c in range(N_CHUNKS):
            states.append(st.astype(BF16))
            st = st * dec[c:c + 1, sl] + kv_all[:, c * A_HEAD_DIM:(c + 1) * A_HEAD_DIM]
        st_s[h] = st
        per = INTRA // CHUNK
        inter = jnp.concatenate(
            [_chunk_diag_blocks(_dot_nt(qb[_intra_rows(i), sl], jnp.concatenate(states[i * per:(i + 1) * per], axis=0)),
                                A_HEAD_DIM) for i in range(TS // INTRA)], axis=0)
        o = jnp.concatenate(intra, axis=0) + inter
        z = u[:, 3 * WIDTH + h * A_HEAD_DIM:3 * WIDTH + (h + 1) * A_HEAD_DIM]
        o_ref[:, sl] = (_rms_rows(o, ng[:, sl]) * _silu(z)).astype(o_ref.dtype)


def _ret_kernel(x_ref, g_ref, w_ref, dmat_ref, kdec_ref, qdec_ref, cdec_ref, ng_ref, o_ref, st_s):
    @pl.when(pl.program_id(1) == 0)
    def _():
        st_s[...] = jnp.zeros_like(st_s)

    u = jnp.dot(_normed_input(x_ref, g_ref), w_ref[...], preferred_element_type=F32)
    nqk = D_HEADS * D_QK
    q = u[:, 0:nqk] * (D_QK ** -0.5)
    k = u[:, nqk:2 * nqk]
    kb = k.astype(BF16)
    kd = (k * kdec_ref[...]).astype(BF16)
    vb = u[:, 2 * nqk:2 * nqk + WIDTH].astype(BF16)
    first_half = _iota((TS, LANES), 1) < D_QK
    qdec = qdec_ref[...]
    cdec = cdec_ref[...]
    ng = ng_ref[...]
    for h in range(D_HEADS):
        pair = slice((h // 2) * LANES, (h // 2 + 1) * LANES)
        vsl = slice(h * D_V, (h + 1) * D_V)
        qm = jnp.where(first_half if h % 2 == 0 else jnp.logical_not(first_half), q[:, pair], 0.0).astype(BF16)
        intra = []
        for i in range(TS // INTRA):
            rows = slice(i * INTRA, (i + 1) * INTRA)
            sc = _dot_nt(qm[rows], kb[rows, pair]) * dmat_ref[h]
            intra.append(_dot(sc, vb[rows, vsl]))
        kv_all = jnp.concatenate([_dot_tn(kd[_intra_rows(i), pair], _chunk_block_diag(vb[_intra_rows(i), vsl]))
                                  for i in range(TS // INTRA)], axis=1)
        st = st_s[h]
        states = []
        for c in range(N_CHUNKS):
            states.append(st.astype(BF16))
            st = st * cdec[:, vsl] + kv_all[:, c * D_V:(c + 1) * D_V]
        st_s[h] = st
        per = INTRA // CHUNK
        inter = jnp.concatenate(
            [_chunk_diag_blocks(_dot(qm[_intra_rows(i)], jnp.concatenate(states[i * per:(i + 1) * per], axis=1)), D_V)
             for i in range(TS // INTRA)], axis=0)
        o = jnp.concatenate(intra, axis=0) + inter * qdec[:, vsl]
        oc = o - jnp.mean(o, axis=-1, keepdims=True)
        var = jnp.mean(oc * oc, axis=-1, keepdims=True)
        z = u[:, 2 * nqk + WIDTH + h * D_V:2 * nqk + WIDTH + (h + 1) * D_V]
        o_ref[:, vsl] = (oc * lax.rsqrt(var + EPS) * ng[:, vsl] * _silu(z)).astype(o_ref.dtype)


CONV_PAD = 8
GROUP = 8


def _scan_steps(a, b, row, length):
    s = 1
    while s < length:
        keep = row >= s
        a_prev = jnp.where(keep, pltpu.roll(a, s, 0), 1.0)
        b_prev = jnp.where(keep, pltpu.roll(b, s, 0), 0.0)
        b = a * b_prev + b
        a = a * a_prev
        s *= 2
    return a, b


def _rglru_kernel(x_ref, g_ref, w_ref, cw_ref, cb_ref, wra_ref, bra_ref, wri_ref, bri_ref, lam_ref,
                  o_ref, ext_s, h_s, a_s, b_s, ca_s, cb_s):
    t = pl.program_id(1)

    @pl.when(t == 0)
    def _():
        ext_s[0:CONV_PAD, :] = jnp.zeros((CONV_PAD, WIDTH), F32)
        h_s[...] = jnp.zeros_like(h_s)

    @pl.when(t > 0)
    def _():
        ext_s[0:CONV_PAD, :] = ext_s[TS:TS + CONV_PAD, :]

    u = jnp.dot(_normed_input(x_ref, g_ref), w_ref[...], preferred_element_type=F32)
    ext_s[CONV_PAD:CONV_PAD + TS, :] = u[:, 0:WIDTH]
    cz = u[:, WIDTH:2 * WIDTH]
    cw = cw_ref[...]
    xc = cb_ref[...] + cw[CONV_WIDTH - 1:CONV_WIDTH, :] * u[:, 0:WIDTH]
    for j in range(CONV_WIDTH - 1):
        back = CONV_WIDTH - 1 - j
        xc = xc + cw[j:j + 1, :] * ext_s[CONV_PAD - back:CONV_PAD - back + TS, :]
    neg_lam = -lam_ref[...]
    softplus = jnp.maximum(neg_lam, 0.0) + jnp.log1p(jnp.exp(-jnp.abs(neg_lam)))
    half = WIDTH // 2
    a_parts, b_parts = [], []
    for i in range(2):
        cs = slice(i * half, (i + 1) * half)
        xh = xc[:, cs]
        r = _sigmoid(_dot(xh, wra_ref[cs, cs]) + bra_ref[:, cs])
        ig = _sigmoid(_dot(xh, wri_ref[cs, cs]) + bri_ref[:, cs])
        a_h = jnp.exp(-RG_C * r * softplus[:, cs])
        a_parts.append(a_h)
        b_parts.append(jnp.sqrt(1.0 - a_h * a_h) * (ig * xh))
    a = jnp.concatenate(a_parts, axis=1)
    b = jnp.concatenate(b_parts, axis=1)
    n_groups = TS // GROUP
    planes = WIDTH // LANES
    rows_of = lambda r: pl.ds(r, n_groups, stride=GROUP)
    for j in range(planes):
        a_s[j] = a[:, j * LANES:(j + 1) * LANES]
        b_s[j] = b[:, j * LANES:(j + 1) * LANES]
    cum_a = [a_s[j, rows_of(0), :] for j in range(planes)]
    cum_b = [b_s[j, rows_of(0), :] for j in range(planes)]
    for r in range(GROUP):
        for j in range(planes):
            if r > 0:
                a_r = a_s[j, rows_of(r), :]
                cum_b[j] = a_r * cum_b[j] + b_s[j, rows_of(r), :]
                cum_a[j] = a_r * cum_a[j]
            ca_s[j, rows_of(r), :] = cum_a[j]
            cb_s[j, rows_of(r), :] = cum_b[j]
    group = _iota((n_groups, WIDTH), 0)
    a_tot, b_tot = _scan_steps(jnp.concatenate(cum_a, axis=1), jnp.concatenate(cum_b, axis=1), group, n_groups)
    h0 = h_s[0:1, :]
    h_after = a_tot * h0 + b_tot
    h_before = jnp.where(group == 0, h0, pltpu.roll(h_after, 1, 0))
    h_s[0:1, :] = h_after[n_groups - 1:n_groups, :]
    for r in range(GROUP):
        for j in range(planes):
            lanes = slice(j * LANES, (j + 1) * LANES)
            a_s[j, rows_of(r), :] = ca_s[j, rows_of(r), :] * h_before[:, lanes] + cb_s[j, rows_of(r), :]
    h = jnp.concatenate([a_s[j] for j in range(planes)], axis=1)
    o_ref[...] = (h * _silu(cz)).astype(o_ref.dtype)


NSA_Q = 0
NSA_KC = 512
NSA_VC = 640
NSA_KS = 768
NSA_VS = 896
NSA_KW = 1024
NSA_VW = 1152
NSA_G = 1280
NSA_Z = 1408
NSA_COLS = 1920
SEL_KEYS = TS
WIN_Q = 256
WIN_KEYS = Q_BLOCK + WINDOW
CMP_PHASES = CMP_STRIDE
AUG = 2 * LANES
POS_ROWS = 16
MASK_BIG = float(2.0 ** 40)
SUM_ROWS = 16
SEL_COLS = 256
SEL_AHEAD = 3
T_ROWS = (6, 7, 8)
FIXED_SHIFT_RANGE = 60.0


def _half_rms(v, first_half):
    v2 = v * v
    s0 = jnp.sum(jnp.where(first_half, v2, 0.0), axis=-1, keepdims=True)
    s1 = jnp.sum(jnp.where(first_half, 0.0, v2), axis=-1, keepdims=True)
    ms = jnp.where(first_half, s0, s1) * (1.0 / B_HEAD_DIM)
    return v * lax.rsqrt(ms + EPS)


def _rank_select(imp, n_live):
    n_sel, n_query = imp.shape
    cnt = []
    for v0 in range(0, n_live, 8):
        mine, jmine = imp[v0:v0 + 8], v0 + _iota((8, n_query), 0)
        c = jnp.zeros((8, n_query), F32)
        for jp in range(n_live):
            other = imp[jp:jp + 1, :]
            if jp < v0:
                ahead = other >= mine
            elif jp >= v0 + 8:
                ahead = other > mine
            else:
                ahead = jnp.where(jmine > jp, jnp.where(other >= mine, 1.0, 0.0),
                                  jnp.where(other > mine, 1.0, 0.0)) > 0.5
            c = c + jnp.where(ahead, 1.0, 0.0)
        cnt.append(jnp.where(c < SEL_TOPN, 0.0, -MASK_BIG))
    if n_live < n_sel:
        cnt.append(jnp.full((n_sel - n_live, n_query), -MASK_BIG, F32))
    return jnp.concatenate(cnt, axis=0).astype(BF16)


def _softmax_cols(s):
    p = jnp.exp2(s - jnp.max(s, axis=0, keepdims=True))
    return p, jnp.sum(p, axis=0, keepdims=True)


def _nsa_kernel(seq, online, x_ref, g_ref, w_ref, qg_ref, kg_ref, pos12_ref, w12k_ref, w12v_ref,
                ov_ref, slope_ref, slopef_ref, possel_ref, poswin_ref, poscmp_ref, o_ref,
                u_s, ks_c, vst_c, kw_c, vwt_c, kc_c, vc_c, p1k_s, p1v_s, kv_s):
    n_cmp = seq // CMP_STRIDE
    n_sel = seq // SEL_LEN
    ti = pl.program_id(1)

    @pl.when(ti == 0)
    def _():
        for ref in (ks_c, vst_c, kw_c, vwt_c, kc_c, vc_c, p1k_s, p1v_s):
            ref[...] = jnp.zeros_like(ref)

    u_s[...] = jnp.dot(_normed_input(x_ref, g_ref), w_ref[...], preferred_element_type=F32)
    row0 = pl.multiple_of(ti * TS, TS)
    kg = kg_ref[...]
    half_t = _iota((TS, LANES), 1) < B_HEAD_DIM

    ks_c[pl.ds(row0, TS), :] = (_half_rms(u_s[:, NSA_KS:NSA_KS + B_KV], half_t) * kg).astype(BF16)
    vst_c[ti] = u_s[:, NSA_VS:NSA_VS + B_KV].T.astype(BF16)
    kw_c[pl.ds(row0 + WINDOW, TS), :] = (_half_rms(u_s[:, NSA_KW:NSA_KW + B_KV], half_t) * kg).astype(BF16)
    for i in range(TS // Q_BLOCK):
        vwt_c[ti * (TS // Q_BLOCK) + WINDOW // Q_BLOCK + i] = (
            u_s[i * Q_BLOCK:(i + 1) * Q_BLOCK, NSA_VW:NSA_VW + B_KV].T.astype(BF16))

    n_grp = TS // CMP_STRIDE
    first_row = _iota((n_grp, B_KV), 0) == 0
    half_g = _iota((n_grp, LANES), 1) < B_HEAD_DIM
    crow = pl.ds(pl.multiple_of(ti * n_grp, n_grp), n_grp)
    for i, (col, w12_ref, p1_s, cache, is_key) in enumerate(((NSA_KC, w12k_ref, p1k_s, kc_c, True),
                                                             (NSA_VC, w12v_ref, p1v_s, vc_c, False))):
        kv_s[i] = u_s[:, col:col + B_KV]
        grouped = jnp.concatenate([kv_s[i, pl.ds(r, n_grp, stride=CMP_STRIDE), :] for r in range(CMP_PHASES)],
                                  axis=1)
        both = _dot(jnp.concatenate([grouped, pos12_ref[...]], axis=0), w12_ref[...])
        p1 = both[0:n_grp, 0:B_KV] + both[n_grp:n_grp + 1, 0:B_KV]
        p2 = both[0:n_grp, B_KV:2 * B_KV] + both[n_grp + 1:n_grp + 2, B_KV:2 * B_KV]
        p1_prev = jnp.where(first_row, p1_s[n_grp - 1:n_grp, :], pltpu.roll(p1, 1, 0))
        p1_s[...] = p1
        blk = p1_prev + p2
        if is_key:
            blk = _half_rms(blk, half_g) * kg
        cache[crow, :] = blk.astype(BF16)

    qg = qg_ref[...]
    n_q = B_REP * TS
    halves = TS // WIN_Q

    def cols(a):
        return jnp.concatenate([a[:, hb * WIN_Q:(hb + 1) * WIN_Q] for hb in range(halves) for _ in range(B_REP)],
                               axis=1)

    def col_block(hb, r):
        return slice((hb * B_REP + r) * WIN_Q, (hb * B_REP + r + 1) * WIN_Q)

    q_n = [_half_rms(u_s[:, NSA_Q + r * LANES:NSA_Q + (r + 1) * LANES], half_t)
           * qg[:, r * LANES:(r + 1) * LANES] * (B_HEAD_DIM ** -0.5 * LOG2E) for r in range(B_REP)]
    q_t = jnp.concatenate([q_n[r][hb * WIN_Q:(hb + 1) * WIN_Q].T for hb in range(halves) for r in range(B_REP)],
                          axis=1)
    q_rows_first = _iota((LANES, n_q), 0) < B_HEAD_DIM
    t_col = (row0 + cols(_iota((1, TS), 1))).astype(F32)
    prow = _iota((POS_ROWS, n_q), 0)
    slope_rows = []
    for g in range(B_GROUPS):
        rest = -slopef_ref[g] * t_col
        block = slope_ref[g]
        for row in T_ROWS:
            piece = rest.astype(BF16).astype(F32)
            block = block + jnp.where(prow == row, piece, 0.0)
            rest = rest - piece
        slope_rows.append(block.astype(BF16))
    base = [jnp.concatenate([jnp.where(q_rows_first, q_t, 0.0).astype(BF16), slope_rows[0]], axis=0),
            jnp.concatenate([jnp.where(q_rows_first, 0.0, q_t).astype(BF16), slope_rows[1]], axis=0)]
    n_base = LANES + POS_ROWS
    groups = range(B_GROUPS)

    t_c = row0 + _iota((n_cmp, TS), 1)
    nprime = _iota((n_cmp, TS), 0)
    madd_c = cols(jnp.where(t_c >= nprime * CMP_STRIDE + (CMP_STRIDE - 1),
                            jnp.where(nprime >= 1, 0.0, NEG_INF), NEG_INF))
    any_c = cols(jnp.where(row0 + _iota((1, TS), 1) >= CMP_LEN - 1, 1.0, 0.0))
    kaug_c = jnp.concatenate([kc_c[...], poscmp_ref[...]], axis=1)
    vct = vc_c[...].astype(F32).T.astype(BF16)
    zeros_c = jnp.zeros((AUG - n_base, n_q), BF16)
    s_c = [jnp.dot(kaug_c, jnp.concatenate([base[g], zeros_c], axis=0), preferred_element_type=F32) + madd_c
           for g in groups]
    p_c = []
    for g in groups:
        p, l = _softmax_cols(s_c[g])
        p_c.append(p * (any_c / l))
    o_cmp = [_dot(vct[g * B_HEAD_DIM:(g + 1) * B_HEAD_DIM], p_c[g]) for g in groups]
    blk_t = (row0 + _iota((n_sel, TS), 1)) >> 6
    jrow = _iota((n_sel, TS), 0)
    q_aug = []
    for g in groups:
        p_sum = jnp.concatenate(
            [sum(p_c[g][:, col_block(hb, r)] for r in range(B_REP)) for hb in range(halves)], axis=1)
        imp = jnp.dot(ov_ref[...], p_sum, precision=lax.Precision.HIGHEST,
                      preferred_element_type=F32)
        imp = jnp.where(jrow == blk_t, BIG, jnp.where(jrow < blk_t, imp, -BIG))
        sel_neg = lax.switch(ti, [functools.partial(_rank_select, n_live=(k + 1) * (TS // SEL_LEN))
                                  for k in range(seq // TS)], imp)
        q_aug.append(jnp.concatenate([base[g], cols(sel_neg),
                                      jnp.zeros((AUG - n_base - n_sel, n_q), BF16)], axis=0))

    def sel_step(kc, st, diagonal):
        k0 = pl.multiple_of(kc * SEL_KEYS, SEL_KEYS)
        kaug = jnp.concatenate([ks_c[pl.ds(k0, SEL_KEYS), :], possel_ref[pl.ds(k0, SEL_KEYS), :]], axis=1)
        ones = jnp.ones((SUM_ROWS, SEL_KEYS), BF16)
        v_aug = [jnp.concatenate([vst_c[kc, g * B_HEAD_DIM:(g + 1) * B_HEAD_DIM, :], ones], axis=0) for g in groups]
        if diagonal:
            madd = cols(jnp.where(_iota((SEL_KEYS, TS), 0) <= _iota((SEL_KEYS, TS), 1), 0.0, NEG_INF))
        units = [(g, slice(b * SEL_COLS, (b + 1) * SEL_COLS)) for g in groups for b in range(n_q // SEL_COLS)]

        def n_keys(cs):
            return (cs.start // (B_REP * WIN_Q) + 1) * WIN_Q if diagonal else SEL_KEYS

        def qk(g, cs):
            n = n_keys(cs)
            s = jnp.dot(kaug[0:n], q_aug[g][:, cs], preferred_element_type=F32)
            return s + madd[0:n, cs] if diagonal else s

        scores = [qk(*u) for u in units[:SEL_AHEAD]]
        m_out, acc_out = [[] for _ in groups], [[] for _ in groups]
        for i, (g, cs) in enumerate(units):
            if i + SEL_AHEAD < len(units):
                scores.append(qk(*units[i + SEL_AHEAD]))
            s = scores[i]
            m_old, acc_old = st[2 * g][:, cs], st[2 * g + 1][:, cs]
            if online:
                m_new = jnp.maximum(m_old, jnp.max(s, axis=0, keepdims=True))
                s = s - m_new
                acc_old = jnp.exp2(m_old - m_new) * acc_old
            else:
                m_new = m_old
            p = jnp.exp2(s.astype(BF16))
            m_out[g].append(m_new)
            acc_out[g].append(acc_old + jnp.dot(v_aug[g][:, 0:n_keys(cs)], p, preferred_element_type=F32))
        out = []
        for g in groups:
            out += [jnp.concatenate(m_out[g], axis=1), jnp.concatenate(acc_out[g], axis=1)]
        return tuple(out)

    init = []
    for g in groups:
        init += [jnp.full((1, n_q), NEG_INF, F32), jnp.zeros((B_HEAD_DIM + SUM_ROWS, n_q), F32)]
    st = lax.fori_loop(0, ti, functools.partial(sel_step, diagonal=False), tuple(init))
    st = sel_step(ti, st, diagonal=True)
    o_sel = [st[2 * g + 1][0:B_HEAD_DIM] / st[2 * g + 1][B_HEAD_DIM:B_HEAD_DIM + 1] for g in groups]

    per_half = WIN_Q // Q_BLOCK
    win_ones = jnp.ones((SUM_ROWS, WIN_KEYS), BF16)
    head_pairs = B_REP // 2

    def win_scores(qi, g, hp):
        hb, qq = divmod(qi, per_half)
        w0 = row0 + qi * Q_BLOCK
        kaug_w = jnp.concatenate([kw_c[pl.ds(w0, WIN_KEYS), :], poswin_ref[pl.ds(w0, WIN_KEYS), :]], axis=1)
        spos = (w0 - WINDOW) + _iota((WIN_KEYS, Q_BLOCK), 0)
        dist_w = (w0 + _iota((WIN_KEYS, Q_BLOCK), 1)) - spos
        madd_w = jnp.where(dist_w >= 0, jnp.where(dist_w < WINDOW, jnp.where(spos >= 0, 0.0, NEG_INF), NEG_INF),
                           NEG_INF)
        q_w = jnp.concatenate([q_aug[g][:, col_block(hb, r).start + qq * Q_BLOCK:
                                        col_block(hb, r).start + (qq + 1) * Q_BLOCK]
                               for r in (2 * hp, 2 * hp + 1)], axis=1)
        return (jnp.dot(kaug_w, q_w, preferred_element_type=F32)
                + jnp.concatenate([madd_w] * 2, axis=1))

    win_units = [(qi, g, hp) for qi in range(TS // Q_BLOCK) for hp in range(head_pairs) for g in groups]
    win_s = [win_scores(*u) for u in win_units[:SEL_AHEAD]]
    o_win_unit = {}
    for i, (qi, g, hp) in enumerate(win_units):
        if i + SEL_AHEAD < len(win_units):
            win_s.append(win_scores(*win_units[i + SEL_AHEAD]))
        s = win_s[i]
        if online:
            s = s - jnp.max(s, axis=0, keepdims=True)
        wblk = ti * (TS // Q_BLOCK) + qi
        vwt = jnp.concatenate(
            [jnp.concatenate([vwt_c[wblk + j, g * B_HEAD_DIM:(g + 1) * B_HEAD_DIM, :]
                              for j in range(WIN_KEYS // Q_BLOCK)], axis=1), win_ones], axis=0)
        pv = jnp.dot(vwt, jnp.exp2(s.astype(BF16)), preferred_element_type=F32)
        o_win_unit[(qi, g, hp)] = pv[0:B_HEAD_DIM] / pv[B_HEAD_DIM:B_HEAD_DIM + 1]
    o_win = [jnp.concatenate(
        [o_win_unit[(hb * per_half + qq, g, r // 2)][:, (r % 2) * Q_BLOCK:(r % 2 + 1) * Q_BLOCK]
         for hb in range(halves) for r in range(B_REP) for qq in range(per_half)], axis=1) for g in groups]

    gates_t = _sigmoid(u_s[:, NSA_G:NSA_G + LANES]).T
    for hb in range(halves):
        ts = slice(hb * WIN_Q, (hb + 1) * WIN_Q)
        for r in range(B_REP):
            cs = col_block(hb, r)
            acc = jnp.zeros((LANES, WIN_Q), F32)
            for c, branch in enumerate((o_cmp, o_sel, o_win)):
                acc = acc + jnp.concatenate(
                    [branch[0][:, cs] * gates_t[r * 3 + c:r * 3 + c + 1, ts],
                     branch[1][:, cs] * gates_t[(B_REP + r) * 3 + c:(B_REP + r) * 3 + c + 1, ts]], axis=0)
            z = u_s[ts, NSA_Z + r * LANES:NSA_Z + (r + 1) * LANES]
            o_ref[ts, r * LANES:(r + 1) * LANES] = (acc.T * _silu(z)).astype(o_ref.dtype)


def _merge_kernel(x_ref, oa_ref, ob_ref, oc_ref, od_ref, g_ref, wmg_ref, mb_ref, wbr_ref, wout_ref, o_ref):
    x = x_ref[...]
    xn = _rms_rows(x, g_ref[...]).astype(BF16)
    merged = jnp.zeros((TS, D_MODEL), F32)
    for br, o_k in enumerate((oa_ref, ob_ref, oc_ref, od_ref)):
        cols = slice(br * D_MODEL, (br + 1) * D_MODEL)
        gate = _sigmoid(jnp.dot(xn, wmg_ref[:, cols], preferred_element_type=F32) + mb_ref[:, cols])
        merged = merged + gate * jnp.dot(o_k[...], wbr_ref[br], preferred_element_type=F32)
    o_ref[...] = x + jnp.dot(merged.astype(BF16), wout_ref[...], preferred_element_type=F32)


def _const_spec(shape):
    nd = len(shape)
    return pl.BlockSpec(shape, lambda b, t: (0,) * nd, pipeline_mode=pl.Buffered(1))


def _tile_spec(width):
    return pl.BlockSpec((None, TS, width), lambda b, t: (b, t, 0))


def _call(kernel, name, x, consts, out_width, out_dtype, scratch, extra_tiles=()):
    batch, seq, _ = x.shape
    in_specs = [_tile_spec(D_MODEL)] + [_tile_spec(a.shape[-1]) for a in extra_tiles]
    in_specs += [_const_spec(c.shape) for c in consts]
    return pl.pallas_call(
        kernel,
        name=name,
        grid=(batch, seq // TS),
        in_specs=in_specs,
        out_specs=_tile_spec(out_width),
        out_shape=jax.ShapeDtypeStruct((batch, seq, out_width), out_dtype),
        scratch_shapes=scratch,
        compiler_params=pltpu.CompilerParams(
            dimension_semantics=("arbitrary", "arbitrary"), vmem_limit_bytes=VMEM_LIMIT),
    )(x, *extra_tiles, *consts)


def _retention_tables():
    pos = np.arange(TS, dtype=np.float64)
    log_g = np.log1p(-np.exp2(-5.0 - np.arange(D_HEADS, dtype=np.float64)))
    rel = pos[:, None] - pos[None, :]
    same = (pos[:, None] // CHUNK) == (pos[None, :] // CHUNK)
    dmat = np.where((rel >= 0) & same, np.exp(log_g[:, None, None] * np.maximum(rel, 0.0)), 0.0)
    dmat = dmat[:, :INTRA, :INTRA]
    inpos = pos % CHUNK
    kdec = np.repeat(np.exp(log_g[None, :] * (CHUNK - 1.0 - inpos)[:, None]), D_QK, axis=1)
    qdec = np.repeat(np.exp(log_g[None, :] * (inpos + 1.0)[:, None]), D_V, axis=1)
    cdec = np.repeat(np.exp(log_g * CHUNK)[None, :], D_V, axis=1)
    return tuple(jnp.asarray(a, F32) for a in (dmat, kdec, qdec, cdec))


def _bf16_pieces(v, n):
    out = []
    rem = np.asarray(v, np.float64)
    for _ in range(n):
        piece = rem.astype(BF16).astype(np.float64)
        out.append(piece)
        rem = rem - piece
    return out


def _position_tile(pos, block_ids=None):
    pos = np.maximum(np.asarray(pos), 0)
    tab = np.zeros((pos.shape[0], LANES), np.float32)
    for i in range(3):
        tab[:, 2 * i] = (pos // 256) * 256
        tab[:, 2 * i + 1] = pos % 256
    tab[:, list(T_ROWS)] = 1.0
    if block_ids is not None:
        tab[np.arange(pos.shape[0]), POS_ROWS + np.asarray(block_ids)] = 1.0
    return jnp.asarray(tab, BF16)


def _nsa_tables(seq):
    n_cmp = seq // CMP_STRIDE
    n_sel = seq // SEL_LEN
    n = np.arange(n_cmp) - 1
    cmp_start = n * CMP_STRIDE
    sel_start = np.arange(n_sel) * SEL_LEN
    ov = ((cmp_start[None, :] < sel_start[:, None] + SEL_LEN) & (cmp_start[None, :] + CMP_LEN > sel_start[:, None])
          & (n[None, :] >= 0) & (n[None, :] <= (seq - CMP_LEN) // CMP_STRIDE))
    slope = np.zeros((B_GROUPS, POS_ROWS, TS // WIN_Q, B_REP, WIN_Q), np.float32)
    for g in range(B_GROUPS):
        for r in range(B_REP):
            pieces = _bf16_pieces(B_SLOPES[g * B_REP + r] * LOG2E, 3)
            for i, piece in enumerate(pieces):
                slope[g, 2 * i:2 * i + 2, :, r, :] = piece
    slope = slope.reshape(B_GROUPS, POS_ROWS, B_REP * TS)
    slopef = np.zeros((B_GROUPS, 1, TS // WIN_Q, B_REP, WIN_Q), np.float32)
    for g in range(B_GROUPS):
        for r in range(B_REP):
            slopef[g, 0, :, r, :] = B_SLOPES[g * B_REP + r] * LOG2E
    slopef = slopef.reshape(B_GROUPS, 1, B_REP * TS)
    tpos = np.arange(seq)
    possel = _position_tile(tpos, tpos // SEL_LEN)
    poswin = _position_tile(np.arange(seq + WINDOW) - WINDOW)
    poscmp = _position_tile(np.arange(n_cmp) * CMP_STRIDE + (CMP_STRIDE - 1))
    return (jnp.asarray(ov, F32), jnp.asarray(slope, F32), jnp.asarray(slopef), possel, poswin, poscmp)


def _perm_heads(w):
    lead = w.shape[:-1]
    return w.reshape(lead + (B_HEADS, B_HEAD_DIM))[..., B_HEAD_PERM, :].reshape(lead + (WIDTH,))


def _cmp_weights(w):
    w = w.reshape(2, CMP_PHASES, B_HEAD_DIM, B_HEAD_DIM)
    eye = jnp.eye(B_GROUPS, dtype=w.dtype)
    big = jnp.einsum('spde,gh->spgdhe', w, eye).reshape(2, CMP_PHASES * B_KV, B_KV)
    return jnp.concatenate([big[0], big[1]], axis=1).astype(BF16)


def _cmp_pos(pos):
    p = jnp.tile(pos.reshape(2, CMP_PHASES, 1, B_HEAD_DIM), (1, 1, B_GROUPS, 1)).reshape(2, CMP_PHASES * B_KV)
    return jnp.concatenate([p, jnp.zeros((14, CMP_PHASES * B_KV), p.dtype)], axis=0)


def _block_diag(w):
    eye = jnp.eye(C_BLOCKS, dtype=w.dtype)
    return jnp.einsum('ncd,nm->ncmd', w, eye).reshape(WIDTH, WIDTH).astype(BF16)


def kernel(x, norm_g, w_in, lb_logits, a_norm_g, b_q_norm_g, b_k_norm_g, b_cmp_pos, b_cmp_wk, b_cmp_wv,
           c_conv_w, c_conv_b, c_w_ra, c_b_ra, c_w_ri, c_b_ri, c_lambda, d_norm_g, merge_b, w_branch, w_out):
    batch, seq, _ = x.shape
    depth = norm_g.shape[0]
    assert seq % TS == 0 and x.shape[-1] == D_MODEL

    p_lb = jax.nn.softmax(lb_logits.astype(F32), axis=0)
    lower_bounds = jnp.cumsum(p_lb, axis=0) - p_lb[0:1]
    dmat, kdec, qdec, cdec = _retention_tables()
    nsa_tabs = _nsa_tables(seq)
    row = lambda v: v.reshape(1, -1).astype(F32)
    vm = lambda shape, dt=F32: pltpu.VMEM(shape, dt)
    n_cmp = seq // CMP_STRIDE

    for l in range(depth):
        w = w_in[l]
        g = row(norm_g[l])
        w_a = w[:, 0:2048].astype(BF16)
        b0 = 2048
        gate_cols = jnp.pad(w[:, b0 + 1280:b0 + 1304], ((0, 0), (0, LANES - 3 * B_HEADS)))
        w_b = jnp.concatenate([_perm_heads(w[:, b0:b0 + 512]), w[:, b0 + 512:b0 + 1280], gate_cols,
                               _perm_heads(w[:, b0 + 1304:b0 + 1816])], axis=1).astype(BF16)
        c0 = b0 + 1816
        w_c = w[:, c0:c0 + 1024].astype(BF16)
        d0 = c0 + 1024
        w_d = w[:, d0:d0 + 1536].astype(BF16)
        m0 = d0 + 1536
        w_m = w[:, m0:m0 + N_BRANCH * D_MODEL].astype(BF16)

        o_a = _call(_hgrn_kernel, "hgrn", x, [g, w_a, row(lower_bounds[l]), row(a_norm_g[l])], WIDTH, BF16,
                    [vm((A_HEADS, A_HEAD_DIM, A_HEAD_DIM))])

        nsa_consts = [g, w_b, row(jnp.tile(b_q_norm_g[l], B_HEADS)), row(jnp.tile(b_k_norm_g[l], B_GROUPS)),
                      _cmp_pos(b_cmp_pos[l]), _cmp_weights(b_cmp_wk[l]), _cmp_weights(b_cmp_wv[l]), *nsa_tabs]
        nsa_scratch = [vm((TS, NSA_COLS)), vm((seq, B_KV), BF16), vm((seq // TS, B_KV, TS), BF16),
                       vm((seq + WINDOW, B_KV), BF16), vm(((seq + WINDOW) // Q_BLOCK, B_KV, Q_BLOCK), BF16),
                       vm((n_cmp, B_KV), BF16), vm((n_cmp, B_KV), BF16),
                       vm((TS // CMP_STRIDE, B_KV)), vm((TS // CMP_STRIDE, B_KV)), vm((2, TS, B_KV))]
        nsa = lambda online: _call(functools.partial(_nsa_kernel, seq, online), "nsa_online" if online else "nsa",
                                   x, nsa_consts, WIDTH, BF16, nsa_scratch)
        score_bound = (B_HEAD_DIM * B_HEAD_DIM ** -0.5 * LOG2E * jnp.max(jnp.abs(b_q_norm_g[l]))
                       * jnp.max(jnp.abs(b_k_norm_g[l])))
        o_b = lax.cond(score_bound <= FIXED_SHIFT_RANGE, lambda: nsa(False), lambda: nsa(True))

        o_c = _call(_rglru_kernel, "rglru", x,
                    [g, w_c, c_conv_w[l].astype(F32), row(c_conv_b[l]), _block_diag(c_w_ra[l]), row(c_b_ra[l]),
                     _block_diag(c_w_ri[l]), row(c_b_ri[l]), row(c_lambda[l])], WIDTH, BF16,
                    [vm((TS + 2 * CONV_PAD, WIDTH)), vm((8, WIDTH))] + [vm((WIDTH // LANES, TS, LANES))] * 4)

        o_d = _call(_ret_kernel, "retention", x, [g, w_d, dmat, kdec, qdec, cdec, row(d_norm_g[l])], WIDTH, BF16,
                    [vm((D_HEADS, LANES, D_V))])

        wbr = w_branch[l]
        wbr = jnp.stack([wbr[0], _perm_heads(wbr[1].T).T, wbr[2], wbr[3]], axis=0).astype(BF16)
        x = _call(_merge_kernel, "merge", x, [g, w_m, row(merge_b[l]), wbr, w_out[l].astype(BF16)],
                  D_MODEL, x.dtype, [], extra_tiles=(o_a, o_b, o_c, o_d))
    return x
```
